```python
import math
import jax, jax.numpy as jnp
from jax import lax
import numpy as np

D_MODEL = 2048
BATCH = 4
SEQ = 4096
DEPTH = 4

N_META = 16
ATTN_WIDTH = D_MODEL // 2
ATTN_HEAD_DIM = 128
ATTN_HEADS = ATTN_WIDTH // ATTN_HEAD_DIM
SSM_WIDTH = D_MODEL // 4
SSM_GROUP_CH = 16
SSM_GROUPS = SSM_WIDTH // SSM_GROUP_CH
SSM_STATE = 64
CONV_CH = D_MODEL // 4
CONV_K = 31
MIX_WIDTH = ATTN_WIDTH + SSM_WIDTH + CONV_CH
IN_PROJ = 3 * ATTN_WIDTH + SSM_WIDTH + 2 * CONV_CH
D_FF = 11 * D_MODEL // 4
N_EXPERTS = 8
TOP_K = 2
Q_BLOCK = 128
N_DENSE = (DEPTH + 1) // 2
N_MOE = DEPTH // 2
EPS = 1e-6

kernel_name = "hymba_s5_conformer_stickbreak_moe"


def rms_norm(x, g):
    xf = x.astype(jnp.float32)
    y = xf * lax.rsqrt(jnp.mean(xf * xf, axis=-1, keepdims=True) + EPS)
    return (y * g.astype(jnp.float32)).astype(x.dtype)


def layer_norm(x, g, b):
    xf = x.astype(jnp.float32)
    mu = jnp.mean(xf, axis=-1, keepdims=True)
    var = jnp.mean(jnp.square(xf - mu), axis=-1, keepdims=True)
    y = (xf - mu) * lax.rsqrt(var + EPS)
    return (y * g.astype(jnp.float32) + b.astype(jnp.float32)).astype(x.dtype)


def swiglu(h, w_gate, w_up, w_down):
    return (jax.nn.silu(h @ w_gate) * (h @ w_up)) @ w_down


def stick_breaking_attention(q, k, v):
    bsz, L, H, Dh = q.shape
    scale = Dh ** -0.5
    key_pos = jnp.arange(L, dtype=jnp.int32)

    def attend(q_blk, q_pos):
        z = jnp.einsum('bqhd,bkhd->bhqk', q_blk, k).astype(jnp.float32) * scale
        mask = key_pos[None, :] < q_pos[:, None]
        log_beta = jax.nn.log_sigmoid(z)
        log_1m = jnp.where(mask, jax.nn.log_sigmoid(-z), 0.0)
        after = lax.cumsum(log_1m, axis=3, reverse=True) - log_1m
        w = jnp.where(mask, jnp.exp(log_beta + after), 0.0)
        return jnp.einsum('bhqk,bkhd->bqhd', w.astype(v.dtype), v)

    out_meta = attend(q[:, :N_META], jnp.arange(N_META, dtype=jnp.int32))
    n_blocks = (L - N_META) // Q_BLOCK
    q_real = q[:, N_META:].reshape(bsz, n_blocks, Q_BLOCK, H, Dh).transpose(1, 0, 2, 3, 4)
    pos_real = (N_META + jnp.arange(L - N_META, dtype=jnp.int32)).reshape(n_blocks, Q_BLOCK)
    out_real = lax.map(lambda qp: attend(qp[0], qp[1]), (q_real, pos_real))
    out_real = out_real.transpose(1, 0, 2, 3, 4).reshape(bsz, L - N_META, H, Dh)
    return jnp.concatenate([out_meta, out_real], axis=1)


def s5_mixer(u, lam_re, lam_im, log_dt, b_re, b_im, c_re, c_im, d_skip, w_glu):
    bsz, L, _ = u.shape
    uf = u.astype(jnp.float32)
    ug = uf.reshape(bsz, L, SSM_GROUPS, SSM_GROUP_CH)
    dt = jnp.exp(log_dt.astype(jnp.float32))[:, None]
    lr = lam_re.astype(jnp.float32)
    li = lam_im.astype(jnp.float32)
    mag = jnp.exp(lr * dt)
    ang = li * dt
    ab_re = mag * jnp.cos(ang)
    ab_im = mag * jnp.sin(ang)
    den = lr * lr + li * li
    nr = ab_re - 1.0
    ni = ab_im
    coef_re = (nr * lr + ni * li) / den
    coef_im = (ni * lr - nr * li) / den
    br = b_re.astype(jnp.float32)
    bi = b_im.astype(jnp.float32)
    bbar_re = coef_re[..., None] * br - coef_im[..., None] * bi
    bbar_im = coef_re[..., None] * bi + coef_im[..., None] * br
    bu_re = jnp.einsum('blgh,gph->blgp', ug, bbar_re)
    bu_im = jnp.einsum('blgh,gph->blgp', ug, bbar_im)
    a_re = jnp.broadcast_to(ab_re, bu_re.shape)
    a_im = jnp.broadcast_to(ab_im, bu_re.shape)

    def combine(e1, e2):
        a1r, a1i, b1r, b1i = e1
        a2r, a2i, b2r, b2i = e2
        return (a2r * a1r - a2i * a1i,
                a2r * a1i + a2i * a1r,
                a2r * b1r - a2i * b1i + b2r,
                a2r * b1i + a2i * b1r + b2i)

    _, _, s_re, s_im = lax.associative_scan(combine, (a_re, a_im, bu_re, bu_im), axis=1)
    y = (jnp.einsum('blgp,ghp->blgh', s_re, c_re.astype(jnp.float32))
         - jnp.einsum('blgp,ghp->blgh', s_im, c_im.astype(jnp.float32)))
    y = y.reshape(bsz, L, SSM_WIDTH) + d_skip.astype(jnp.float32) * uf
    g = jax.nn.gelu(y)
    out = g * jax.nn.sigmoid(g @ w_glu.astype(jnp.float32))
    return out.astype(u.dtype)


def conv_module(val, gate, w_dw, b_dw, ln_g, ln_b, w_pw):
    h = val * jax.nn.sigmoid(gate)
    h = lax.conv_general_dilated(
        h, w_dw[:, None, :].astype(h.dtype), window_strides=(1,),
        padding=[(CONV_K - 1, 0)], dimension_numbers=('NWC', 'WIO', 'NWC'),
        feature_group_count=CONV_CH) + b_dw
    h = jax.nn.silu(layer_norm(h, ln_g, ln_b))
    return h @ w_pw


def moe_swiglu(h, w_router, w_gate, w_up, w_down):
    logits = (h @ w_router).astype(jnp.float32)
    top_val, top_idx = lax.top_k(logits, TOP_K)
    top_w = jax.nn.softmax(top_val, axis=-1)
    gates = jnp.sum(jax.nn.one_hot(top_idx, N_EXPERTS, dtype=jnp.float32) * top_w[..., None], axis=-2)
    gates = gates.astype(h.dtype)
    out = jnp.zeros_like(h)
    for e in range(N_EXPERTS):
        out = out + gates[..., e:e + 1] * swiglu(h, w_gate[e], w_up[e], w_down[e])
    return out


def setup_inputs(seed: int = 0) -> dict:
    key = jax.random.key(seed)
    ks = iter(jax.random.split(key, 40))
    f32 = jnp.float32

    def nrm(shape, scale):
        return jax.random.normal(next(ks), shape, f32) * scale

    def gain(shape):
        return 1.0 + 0.02 * jax.random.normal(next(ks), shape, f32)

    n_idx = jnp.arange(SSM_STATE, dtype=f32)
    lam_re = -0.5 * jnp.exp(0.02 * jax.random.normal(next(ks), (DEPTH, SSM_GROUPS, SSM_STATE), f32))
    lam_im = math.pi * n_idx + 0.01 * jax.random.normal(next(ks), (DEPTH, SSM_GROUPS, SSM_STATE), f32)
    log_dt = jax.random.uniform(next(ks), (DEPTH, SSM_GROUPS), f32,
                                minval=math.log(1e-3), maxval=math.log(1e-1))
    return {
        "x": nrm((BATCH, SEQ, D_MODEL), 1.0),
        "meta_tokens": nrm((N_META, D_MODEL), 1.0),
        "norm_mix_g": gain((DEPTH, D_MODEL)),
        "w_in": nrm((DEPTH, D_MODEL, IN_PROJ), D_MODEL ** -0.5),
        "w_out": nrm((DEPTH, MIX_WIDTH, D_MODEL), MIX_WIDTH ** -0.5),
        "attn_norm_g": gain((DEPTH, ATTN_WIDTH)),
        "ssm_lambda_re": lam_re,
        "ssm_lambda_im": lam_im,
        "ssm_log_dt": log_dt,
        "ssm_b_re": nrm((DEPTH, SSM_GROUPS, SSM_STATE, SSM_GROUP_CH), (2 * SSM_GROUP_CH) ** -0.5),
        "ssm_b_im": nrm((DEPTH, SSM_GROUPS, SSM_STATE, SSM_GROUP_CH), (2 * SSM_GROUP_CH) ** -0.5),
        "ssm_c_re": nrm((DEPTH, SSM_GROUPS, SSM_GROUP_CH, SSM_STATE), (2 * SSM_STATE) ** -0.5),
        "ssm_c_im": nrm((DEPTH, SSM_GROUPS, SSM_GROUP_CH, SSM_STATE), (2 * SSM_STATE) ** -0.5),
        "ssm_d": nrm((DEPTH, SSM_WIDTH), 1.0),
        "ssm_w_glu": nrm((DEPTH, SSM_WIDTH, SSM_WIDTH), SSM_WIDTH ** -0.5),
        "ssm_norm_g": gain((DEPTH, SSM_WIDTH)),
        "conv_w_dw": nrm((DEPTH, CONV_K, CONV_CH), CONV_K ** -0.5),
        "conv_b_dw": nrm((DEPTH, CONV_CH), 0.01),
        "conv_ln_g": gain((DEPTH, CONV_CH)),
        "conv_ln_b": nrm((DEPTH, CONV_CH), 0.01),
        "conv_w_pw": nrm((DEPTH, CONV_CH, CONV_CH), CONV_CH ** -0.5),
        "conv_norm_g": gain((DEPTH, CONV_CH)),
        "norm_ffn_g": gain((DEPTH, D_MODEL)),
        "dense_w_gate": nrm((N_DENSE, D_MODEL, D_FF), D_MODEL ** -0.5),
        "dense_w_up": nrm((N_DENSE, D_MODEL, D_FF), D_MODEL ** -0.5),
        "dense_w_down": nrm((N_DENSE, D_FF, D_MODEL), D_FF ** -0.5),
        "moe_w_router": nrm((N_MOE, D_MODEL, N_EXPERTS), D_MODEL ** -0.5),
        "moe_w_gate": nrm((N_MOE, N_EXPERTS, D_MODEL, D_FF), D_MODEL ** -0.5),
        "moe_w_up": nrm((N_MOE, N_EXPERTS, D_MODEL, D_FF), D_MODEL ** -0.5),
        "moe_w_down": nrm((N_MOE, N_EXPERTS, D_FF, D_MODEL), D_FF ** -0.5),
        "final_norm_g": gain((D_MODEL,)),
    }


def reference(x, meta_tokens, norm_mix_g, w_in, w_out, attn_norm_g,
              ssm_lambda_re, ssm_lambda_im, ssm_log_dt, ssm_b_re, ssm_b_im,
              ssm_c_re, ssm_c_im, ssm_d, ssm_w_glu, ssm_norm_g,
              conv_w_dw, conv_b_dw, conv_ln_g, conv_ln_b, conv_w_pw, conv_norm_g,
              norm_ffn_g, dense_w_gate, dense_w_up, dense_w_down,
              moe_w_router, moe_w_gate, moe_w_up, moe_w_down, final_norm_g):
    bsz = x.shape[0]
    meta = jnp.broadcast_to(meta_tokens.astype(x.dtype)[None], (bsz, N_META, D_MODEL))
    h_res = jnp.concatenate([meta, x], axis=1)
    L = h_res.shape[1]
    splits = [ATTN_WIDTH, 2 * ATTN_WIDTH, 3 * ATTN_WIDTH,
              3 * ATTN_WIDTH + SSM_WIDTH, 3 * ATTN_WIDTH + SSM_WIDTH + CONV_CH]

    for i in range(DEPTH):
        h = rms_norm(h_res, norm_mix_g[i])
        proj = h @ w_in[i]
        q, k, v, u_ssm, c_val, c_gate = jnp.split(proj, splits, axis=-1)
        hd = (bsz, L, ATTN_HEADS, ATTN_HEAD_DIM)
        attn = stick_breaking_attention(q.reshape(hd), k.reshape(hd), v.reshape(hd))
        attn = attn.reshape(bsz, L, ATTN_WIDTH)
        ssm = s5_mixer(u_ssm, ssm_lambda_re[i], ssm_lambda_im[i], ssm_log_dt[i],
                       ssm_b_re[i], ssm_b_im[i], ssm_c_re[i], ssm_c_im[i], ssm_d[i], ssm_w_glu[i])
        conv = conv_module(c_val, c_gate, conv_w_dw[i], conv_b_dw[i], conv_ln_g[i],
                           conv_ln_b[i], conv_w_pw[i])
        mixed = jnp.concatenate([rms_norm(attn, attn_norm_g[i]),
                                 rms_norm(ssm, ssm_norm_g[i]),
                                 rms_norm(conv, conv_norm_g[i])], axis=-1)
        h_res = h_res + mixed @ w_out[i]

        h = rms_norm(h_res, norm_ffn_g[i])
        j = i // 2
        if i % 2 == 0:
            ffn = swiglu(h, dense_w_gate[j], dense_w_up[j], dense_w_down[j])
        else:
            ffn = moe_swiglu(h, moe_w_router[j], moe_w_gate[j], moe_w_up[j], moe_w_down[j])
        h_res = h_res + ffn

    return rms_norm(h_res, final_norm_g)[:, N_META:]
```

```python
import functools
import math

import jax
import jax.numpy as jnp
from jax import lax
from jax.experimental import pallas as pl
from jax.experimental.pallas import tpu as pltpu

F32 = jnp.float32
BF16 = jnp.bfloat16
EPS = 1e-6

HEAD_DIM = 128
SSM_GROUP_CH = 16
TAIL_ROWS = 128
ATTN_TILE = 256
SEQ_TILE = 128
SUBLANES = 8
HALO = 32
VMEM_LIMIT = 56 * 1024 * 1024
MOE_TILE = 512
GATHER_ROWS = 256


def _pick(n, prefs):
    for p in prefs:
        if n % p == 0:
            return p
    raise ValueError(f"no tile for {n} in {prefs}")


def _cparams(*sem):
    return pltpu.CompilerParams(dimension_semantics=sem, vmem_limit_bytes=VMEM_LIMIT)


def _dot(a, b):
    return jnp.dot(a, b, preferred_element_type=F32)


def _rms(x, g):
    ms = jnp.mean(x * x, axis=-1, keepdims=True)
    return x * lax.rsqrt(ms + EPS) * g


def _sigmoid(x):
    return 1.0 / (1.0 + jnp.exp(-x))


def _rms_matmul_kernel(x_ref, g_ref, w_ref, s_ref, o_ref, xn_ref):
    @pl.when(pl.program_id(1) == 0)
    def _():
        xn_ref[...] = _rms(x_ref[...], g_ref[...]).astype(BF16)

    o_ref[...] = (_dot(xn_ref[...], w_ref[...]) * s_ref[...]).astype(o_ref.dtype)


def rms_matmul(x, g, w, col_scale, out_dtype):
    R, D = x.shape
    N = w.shape[1]
    tm = _pick(R, (768, 512, 256, 128))
    tn = _pick(N, (768, 512, 384, 256, 128))
    return pl.pallas_call(
        _rms_matmul_kernel,
        grid=(R // tm, N // tn),
        in_specs=[
            pl.BlockSpec((tm, D), lambda i, j: (i, 0)),
            pl.BlockSpec((1, D), lambda i, j: (0, 0)),
            pl.BlockSpec((D, tn), lambda i, j: (0, j)),
            pl.BlockSpec((1, tn), lambda i, j: (0, j)),
        ],
        out_specs=pl.BlockSpec((tm, tn), lambda i, j: (i, j)),
        out_shape=jax.ShapeDtypeStruct((R, N), out_dtype),
        scratch_shapes=[pltpu.VMEM((tm, D), BF16)],
        compiler_params=_cparams("parallel", "arbitrary"),
    )(x, g.reshape(1, D), w, col_scale)


def _sb_block(q, kb, vb, tri, carry, mask):
    z = lax.dot_general(q, kb, (((1,), (1,)), ((), ())), preferred_element_type=F32)
    lb = jnp.minimum(z, 0.0) - jnp.log2(1.0 + jnp.exp2(-jnp.abs(z)))
    l1m = lb - z
    if mask is not None:
        l1m = jnp.where(mask, l1m, 0.0)
    hi = l1m.astype(BF16)
    lo = (l1m - hi.astype(F32)).astype(BF16)
    after = _dot(hi, tri) + _dot(lo, tri)
    w = jnp.exp2(lb + after + carry)
    if mask is not None:
        w = jnp.where(mask, w, 0.0)
    pv = _dot(w.astype(BF16), vb)
    return pv, carry + jnp.sum(l1m, axis=-1, keepdims=True)


def _attn_kernel(q_ref, k_ref, v_ref, tri_ref, o_ref, *, seq, n_meta):
    i = pl.program_id(2)
    tq = ATTN_TILE
    tb = TAIL_ROWS
    n_real = seq // tq
    first_meta = tb - n_meta
    tri = tri_ref[...]
    tri_tail = tri_ref[0:tb, 0:tb]

    @pl.when(i < n_real)
    def _():
        q = q_ref[0]
        r = lax.broadcasted_iota(jnp.int32, (tq, tq), 0)
        c = lax.broadcasted_iota(jnp.int32, (tq, tq), 1)
        off = pl.multiple_of(i * tq, tq)
        acc, carry = _sb_block(q, k_ref[0, pl.ds(off, tq), :], v_ref[0, pl.ds(off, tq), :],
                               tri, jnp.zeros((tq, 1), F32), c < r)

        def body(n, st):
            acc, carry = st
            o = pl.multiple_of((i - 1 - n) * tq, tq)
            pv, carry = _sb_block(q, k_ref[0, pl.ds(o, tq), :], v_ref[0, pl.ds(o, tq), :],
                                  tri, carry, None)
            return acc + pv, carry

        acc, carry = lax.fori_loop(0, i, body, (acc, carry))
        ct = lax.broadcasted_iota(jnp.int32, (tq, tb), 1)
        pv, _ = _sb_block(q, k_ref[0, seq:seq + tb, :], v_ref[0, seq:seq + tb, :],
                          tri_tail, carry, ct >= first_meta)
        o_ref[0] = acc + pv

    @pl.when(i == n_real)
    def _():
        q = q_ref[0, 0:tb, :]
        r = lax.broadcasted_iota(jnp.int32, (tb, tb), 0)
        c = lax.broadcasted_iota(jnp.int32, (tb, tb), 1)
        pv, _ = _sb_block(q, k_ref[0, seq:seq + tb, :], v_ref[0, seq:seq + tb, :],
                          tri_tail, jnp.zeros((tb, 1), F32),
                          jnp.logical_and(c < r, c >= first_meta))
        o_ref[0, 0:tb, :] = pv
        o_ref[0, tb:tq, :] = jnp.zeros((tq - tb, HEAD_DIM), F32)


def attention(qkv, tri, seq, n_meta):
    B, Lp, W3 = qkv.shape
    H = W3 // (3 * HEAD_DIM)
    tq = ATTN_TILE
    nq = seq // tq + 1
    return pl.pallas_call(
        functools.partial(_attn_kernel, seq=seq, n_meta=n_meta),
        grid=(B, H, nq),
        in_specs=[
            pl.BlockSpec((1, tq, HEAD_DIM), lambda b, h, i: (b, i, h)),
            pl.BlockSpec((1, Lp, HEAD_DIM), lambda b, h, i: (b, 0, H + h)),
            pl.BlockSpec((1, Lp, HEAD_DIM), lambda b, h, i: (b, 0, 2 * H + h)),
            pl.BlockSpec((tq, tq), lambda b, h, i: (0, 0)),
        ],
        out_specs=pl.BlockSpec((1, tq, HEAD_DIM), lambda b, h, i: (b, i, h)),
        out_shape=jax.ShapeDtypeStruct((B, Lp, H * HEAD_DIM), F32),
        compiler_params=_cparams("parallel", "parallel", "arbitrary"),
    )(qkv, qkv, qkv, tri)


def _gelu_tanh(x):
    return 0.5 * x * (1.0 + jnp.tanh(math.sqrt(2.0 / math.pi) * (x + 0.044715 * (x * x * x))))


def _s5_kernel(u_ref, bd_ref, cd_ref, pw_ref, d_ref, wglu_ref, g_ref, o_ref,
               carry_ref, st_ref, y_ref, *, nh, cw, cs):
    tc = SEQ_TILE

    @pl.when(pl.program_id(1) == 0)
    def _():
        carry_ref[...] = jnp.zeros_like(carry_ref)

    u = u_ref[0]
    ub = u.astype(BF16)
    for c in range(nh):
        st_ref[...] = _dot(ub[:, c * cw:(c + 1) * cw], bd_ref[c])

        def tile(r, carry, c=c):
            cre, cim = carry
            off = pl.multiple_of(r * SUBLANES, SUBLANES)
            xre = st_ref[pl.ds(off, SUBLANES), 0:cs]
            xim = st_ref[pl.ds(off, SUBLANES), cs:2 * cs]
            for kk, sh in enumerate((1, 2, 4)):
                pr = pw_ref[c, kk, :, 0:cs]
                pi = pw_ref[c, kk, :, cs:2 * cs]
                sre = pltpu.roll(xre, sh, axis=0)
                sim = pltpu.roll(xim, sh, axis=0)
                xre, xim = xre + pr * sre - pi * sim, xim + pr * sim + pi * sre
            pr = pw_ref[c, 3, :, 0:cs]
            pi = pw_ref[c, 3, :, cs:2 * cs]
            xre, xim = xre + pr * cre - pi * cim, xim + pr * cim + pi * cre
            st_ref[pl.ds(off, SUBLANES), 0:cs] = xre
            st_ref[pl.ds(off, SUBLANES), cs:2 * cs] = xim
            return (jnp.broadcast_to(xre[SUBLANES - 1:SUBLANES, :], (SUBLANES, cs)),
                    jnp.broadcast_to(xim[SUBLANES - 1:SUBLANES, :], (SUBLANES, cs)))

        cre, cim = lax.fori_loop(0, tc // SUBLANES, tile,
                                 (carry_ref[c, :, 0:cs], carry_ref[c, :, cs:2 * cs]))
        carry_ref[c, :, 0:cs] = cre
        carry_ref[c, :, cs:2 * cs] = cim
        y_ref[:, c * cw:(c + 1) * cw] = _dot(st_ref[...].astype(BF16), cd_ref[c])

    y = y_ref[...] + d_ref[...] * u
    g = _gelu_tanh(y)
    out = g * _sigmoid(_dot(g.astype(BF16), wglu_ref[...]))
    o_ref[0] = _rms(out, g_ref[...]).astype(BF16)


def s5_mixer(rest, bd, cd, pw, d, wglu, g):
    B, Lp, _ = rest.shape
    nh, cw, cs2 = bd.shape
    cs = cs2 // 2
    SW = nh * cw
    tc = SEQ_TILE
    nblk = Lp // tc
    seq_blk = lambda b, s: (b, (s + nblk - 1) % nblk, 0)
    full = lambda *shape: pl.BlockSpec(shape, lambda b, s: (0,) * len(shape))
    return pl.pallas_call(
        functools.partial(_s5_kernel, nh=nh, cw=cw, cs=cs),
        grid=(B, nblk),
        in_specs=[
            pl.BlockSpec((1, tc, SW), seq_blk),
            full(nh, cw, 2 * cs), full(nh, 2 * cs, cw), full(nh, 4, SUBLANES, 2 * cs),
            full(1, SW), full(SW, SW), full(1, SW),
        ],
        out_specs=pl.BlockSpec((1, tc, SW), seq_blk),
        out_shape=jax.ShapeDtypeStruct((B, Lp, SW), BF16),
        scratch_shapes=[pltpu.VMEM((nh, SUBLANES, 2 * cs), F32),
                        pltpu.VMEM((tc, 2 * cs), F32),
                        pltpu.VMEM((tc, SW), F32)],
        compiler_params=_cparams("parallel", "arbitrary"),
    )(rest, bd, cd, pw, d, wglu, g)


def s5_params(lam_re, lam_im, log_dt, b_re, b_im, c_re, c_im):
    G, P = lam_re.shape
    Hc = SSM_GROUP_CH
    dt = jnp.exp(log_dt.astype(F32))[:, None]
    lr = lam_re.astype(F32)
    li = lam_im.astype(F32)
    mag = jnp.exp(lr * dt)
    ang = li * dt
    ab_re = mag * jnp.cos(ang)
    ab_im = mag * jnp.sin(ang)
    den = lr * lr + li * li
    nr = ab_re - 1.0
    ni = ab_im
    coef_re = (nr * lr + ni * li) / den
    coef_im = (ni * lr - nr * li) / den
    br = b_re.astype(F32)
    bi = b_im.astype(F32)
    bbar_re = coef_re[..., None] * br - coef_im[..., None] * bi
    bbar_im = coef_re[..., None] * bi + coef_im[..., None] * br

    SW = G * Hc
    cw = min(SW, 256)
    gc = cw // Hc
    nh = G // gc
    cs = gc * P
    eye = jnp.eye(gc, dtype=F32)

    def in_blockdiag(m):
        m = m.reshape(nh, gc, P, Hc)
        return jnp.einsum('ngph,gk->nghkp', m, eye).reshape(nh, gc * Hc, gc * P)

    def out_blockdiag(m):
        m = m.reshape(nh, gc, Hc, P)
        return jnp.einsum('nghp,gk->ngpkh', m, eye).reshape(nh, gc * P, gc * Hc)

    bd = jnp.concatenate([in_blockdiag(bbar_re), in_blockdiag(bbar_im)], axis=-1).astype(BF16)
    cd = jnp.concatenate([out_blockdiag(c_re.astype(F32)),
                          -out_blockdiag(c_im.astype(F32))], axis=1).astype(BF16)

    rows = jnp.arange(SUBLANES, dtype=F32)[:, None, None]

    def power(e):
        m = jnp.exp(e * (lr * dt)[None])
        a = e * ang[None]
        return m * jnp.cos(a), m * jnp.sin(a)

    tabs = []
    for sh in (1, 2, 4):
        pr, pi = power(jnp.full((SUBLANES, 1, 1), float(sh), F32))
        keep = rows >= sh
        tabs.append((jnp.where(keep, pr, 0.0), jnp.where(keep, pi, 0.0)))
    tabs.append(power(rows + 1.0))
    pw = jnp.stack([jnp.concatenate([pr.reshape(SUBLANES, nh, cs), pi.reshape(SUBLANES, nh, cs)],
                                    axis=-1) for pr, pi in tabs], axis=0)
    pw = pw.transpose(2, 0, 1, 3)
    return bd, cd, pw


def _conv_kernel(val_ref, gate_ref, hval_ref, hgate_ref, wdw_ref, bdw_ref, lng_ref, lnb_ref,
                 wpw_ref, ng_ref, o_ref, hbuf_ref, *, taps, nblk):
    tc = SEQ_TILE
    halo = hval_ref[0] * _sigmoid(hgate_ref[0])
    halo = jnp.where(pl.program_id(1) == nblk - 1, 0.0, halo)
    hbuf_ref[0:HALO, :] = halo
    hbuf_ref[HALO:HALO + tc, :] = val_ref[0] * _sigmoid(gate_ref[0])
    acc = jnp.broadcast_to(bdw_ref[...], o_ref.shape[1:]).astype(F32)
    base = HALO - (taps - 1)
    for k in range(taps):
        acc = acc + wdw_ref[k:k + 1, :] * hbuf_ref[base + k:base + k + tc, :]
    mu = jnp.mean(acc, axis=-1, keepdims=True)
    xc = acc - mu
    var = jnp.mean(xc * xc, axis=-1, keepdims=True)
    y = xc * lax.rsqrt(var + EPS) * lng_ref[...] + lnb_ref[...]
    y = y * _sigmoid(y)
    out = _dot(y.astype(BF16), wpw_ref[...])
    o_ref[0] = _rms(out, ng_ref[...]).astype(BF16)


def conv_module(rest, col0, wdw, bdw, lng, lnb, wpw, ng):
    B, Lp, _ = rest.shape
    taps, C = wdw.shape
    tc = SEQ_TILE
    nblk = Lp // tc
    vb = col0 // C
    per = tc // HALO
    prev_tail = lambda b, p: ((p + nblk - 1) % nblk) * per + per - 1
    full = lambda *shape: pl.BlockSpec(shape, lambda b, p: (0,) * len(shape))
    return pl.pallas_call(
        functools.partial(_conv_kernel, taps=taps, nblk=nblk),
        grid=(B, nblk),
        in_specs=[
            pl.BlockSpec((1, tc, C), lambda b, p: (b, p, vb)),
            pl.BlockSpec((1, tc, C), lambda b, p: (b, p, vb + 1)),
            pl.BlockSpec((1, HALO, C), lambda b, p: (b, prev_tail(b, p), vb)),
            pl.BlockSpec((1, HALO, C), lambda b, p: (b, prev_tail(b, p), vb + 1)),
            full(taps, C), full(1, C), full(1, C), full(1, C), full(C, C), full(1, C),
        ],
        out_specs=pl.BlockSpec((1, tc, C), lambda b, p: (b, p, 0)),
        out_shape=jax.ShapeDtypeStruct((B, Lp, C), BF16),
        scratch_shapes=[pltpu.VMEM((HALO + tc, C), F32)],
        compiler_params=_cparams("parallel", "parallel"),
    )(rest, rest, rest, rest, wdw, bdw, lng, lnb, wpw, ng)


def _outproj_kernel(attn_ref, ssm_ref, conv_ref, ag_ref, w_ref, res_ref, keep_ref, o_ref,
                    an_ref, *, aw, sw):
    @pl.when(pl.program_id(1) == 0)
    def _():
        an_ref[...] = _rms(attn_ref[...], ag_ref[...]).astype(BF16)

    acc = (_dot(an_ref[...], w_ref[0:aw, :]) + _dot(ssm_ref[...], w_ref[aw:aw + sw, :])
           + _dot(conv_ref[...], w_ref[aw + sw:, :]))
    o_ref[...] = res_ref[...] + jnp.where(keep_ref[...] > 0.0, acc, 0.0)


def out_proj(attn, ssm, conv, ag, w, res, keep):
    R, AW = attn.shape
    SW = ssm.shape[1]
    C = conv.shape[1]
    D = w.shape[1]
    tm = _pick(R, (768, 512, 256, 128))
    tn = _pick(D, (1024, 512, 256, 128))
    return pl.pallas_call(
        functools.partial(_outproj_kernel, aw=AW, sw=SW),
        grid=(R // tm, D // tn),
        in_specs=[
            pl.BlockSpec((tm, AW), lambda i, j: (i, 0)),
            pl.BlockSpec((tm, SW), lambda i, j: (i, 0)),
            pl.BlockSpec((tm, C), lambda i, j: (i, 0)),
            pl.BlockSpec((1, AW), lambda i, j: (0, 0)),
            pl.BlockSpec((AW + SW + C, tn), lambda i, j: (0, j)),
            pl.BlockSpec((tm, tn), lambda i, j: (i, j)),
            pl.BlockSpec((tm, 1), lambda i, j: (i, 0)),
        ],
        out_specs=pl.BlockSpec((tm, tn), lambda i, j: (i, j)),
        out_shape=jax.ShapeDtypeStruct((R, D), F32),
        scratch_shapes=[pltpu.VMEM((tm, AW), BF16)],
        compiler_params=_cparams("parallel", "arbitrary"),
    )(attn, ssm, conv, ag, w, res, keep)


def _ffn_up_kernel(x_ref, g_ref, wg_ref, wu_ref, o_ref, xn_ref):
    @pl.when(pl.program_id(1) == 0)
    def _():
        xn_ref[...] = _rms(x_ref[...], g_ref[...]).astype(BF16)

    a = _dot(xn_ref[...], wg_ref[...])
    o_ref[...] = (a * _sigmoid(a) * _dot(xn_ref[...], wu_ref[...])).astype(BF16)


def ffn_up(x, g, wg, wu):
    R, D = x.shape
    F = wg.shape[1]
    tm = _pick(R, (768, 512, 256, 128))
    tf = _pick(F, (512, 256, 128))
    return pl.pallas_call(
        _ffn_up_kernel,
        grid=(R // tm, F // tf),
        in_specs=[
            pl.BlockSpec((tm, D), lambda i, j: (i, 0)),
            pl.BlockSpec((1, D), lambda i, j: (0, 0)),
            pl.BlockSpec((D, tf), lambda i, j: (0, j)),
            pl.BlockSpec((D, tf), lambda i, j: (0, j)),
        ],
        out_specs=pl.BlockSpec((tm, tf), lambda i, j: (i, j)),
        out_shape=jax.ShapeDtypeStruct((R, F), BF16),
        scratch_shapes=[pltpu.VMEM((tm, D), BF16)],
        compiler_params=_cparams("parallel", "arbitrary"),
    )(x, g.reshape(1, D), wg, wu)


def _ffn_down_kernel(a_ref, w_ref, res_ref, o_ref):
    o_ref[...] = res_ref[...] + _dot(a_ref[...], w_ref[...])


def ffn_down(act, w, res):
    R, F = act.shape
    D = w.shape[1]
    tm = _pick(R, (768, 512, 256, 128))
    tn = _pick(D, (512, 256, 128))
    return pl.pallas_call(
        _ffn_down_kernel,
        grid=(R // tm, D // tn),
        in_specs=[
            pl.BlockSpec((tm, F), lambda i, j: (i, 0)),
            pl.BlockSpec((F, tn), lambda i, j: (0, j)),
            pl.BlockSpec((tm, tn), lambda i, j: (i, j)),
        ],
        out_specs=pl.BlockSpec((tm, tn), lambda i, j: (i, j)),
        out_shape=jax.ShapeDtypeStruct((R, D), F32),
        compiler_params=_cparams("parallel", "parallel"),
    )(act, w, res)


def _router_kernel(x_ref, g_ref, wr_ref, xn_ref, idx_ref, gw_ref, *, n_exp):
    xn = _rms(x_ref[...], g_ref[...])
    xn_ref[...] = xn
    logits = jnp.dot(xn, wr_ref[...], preferred_element_type=F32,
                     precision=lax.Precision.HIGHEST)
    lane = lax.broadcasted_iota(jnp.int32, logits.shape, 1)
    lanef = lane.astype(F32)
    big = float(logits.shape[1])
    logits = jnp.where(lane < n_exp, logits, -jnp.inf)
    m1 = jnp.max(logits, axis=-1, keepdims=True)
    i1 = jnp.min(jnp.where(logits == m1, lanef, big), axis=-1, keepdims=True)
    rest = jnp.where(lanef == i1, -jnp.inf, logits)
    m2 = jnp.max(rest, axis=-1, keepdims=True)
    i2 = jnp.min(jnp.where(rest == m2, lanef, big), axis=-1, keepdims=True)
    e = jnp.exp(m2 - m1)
    w1 = 1.0 / (1.0 + e)
    w2 = e / (1.0 + e)
    idx_ref[...] = jnp.where(lane == 0, i1, jnp.where(lane == 1, i2, 0.0)).astype(jnp.int32)
    gw_ref[...] = jnp.where(lane == 0, w1, jnp.where(lane == 1, w2, 0.0))


def router(x, g, wr_pad, n_exp):
    R, D = x.shape
    tm = _pick(R, (512, 256, 128))
    NL = wr_pad.shape[1]
    return pl.pallas_call(
        functools.partial(_router_kernel, n_exp=n_exp),
        grid=(R // tm,),
        in_specs=[
            pl.BlockSpec((tm, D), lambda i: (i, 0)),
            pl.BlockSpec((1, D), lambda i: (0, 0)),
            pl.BlockSpec((D, NL), lambda i: (0, 0)),
        ],
        out_specs=[
            pl.BlockSpec((tm, D), lambda i: (i, 0)),
            pl.BlockSpec((tm, NL), lambda i: (i, 0)),
            pl.BlockSpec((tm, NL), lambda i: (i, 0)),
        ],
        out_shape=[jax.ShapeDtypeStruct((R, D), F32),
                   jax.ShapeDtypeStruct((R, NL), jnp.int32),
                   jax.ShapeDtypeStruct((R, NL), F32)],
        compiler_params=_cparams("parallel"),
    )(x, g.reshape(1, D), wr_pad)


def _row_copy(src_ref, dst_ref, sem, s, d):
    return pltpu.make_async_copy(src_ref.at[pl.ds(s, 1), :], dst_ref.at[pl.ds(d, 1), :], sem)


def _gather_kernel(idx_ref, src_ref, dst_ref, sem):
    base = pl.program_id(0) * GATHER_ROWS

    def start(r, _):
        _row_copy(src_ref, dst_ref, sem, idx_ref[0, 0, r], base + r).start()
        return 0

    lax.fori_loop(0, GATHER_ROWS, start, 0)

    def wait(r, _):
        _row_copy(src_ref, dst_ref, sem, 0, base + r).wait()
        return 0

    lax.fori_loop(0, GATHER_ROWS, wait, 0)


def gather_rows(src, idx):
    S = idx.shape[0]
    D = src.shape[1]
    n = S // GATHER_ROWS
    return pl.pallas_call(
        _gather_kernel,
        grid=(n,),
        in_specs=[
            pl.BlockSpec((1, 1, GATHER_ROWS), lambda i: (i, 0, 0), memory_space=pltpu.SMEM),
            pl.BlockSpec(memory_space=pl.ANY),
        ],
        out_specs=pl.BlockSpec(memory_space=pl.ANY),
        out_shape=jax.ShapeDtypeStruct((S, D), src.dtype),
        scratch_shapes=[pltpu.SemaphoreType.DMA(())],
        compiler_params=_cparams("arbitrary"),
    )(idx.reshape(n, 1, GATHER_ROWS), src)


def _moe_up_kernel(te_ref, nu_ref, x_ref, wg_ref, wu_ref, o_ref, xb_ref):
    t = pl.program_id(0)

    @pl.when(t < nu_ref[0])
    def _():
        @pl.when(pl.program_id(1) == 0)
        def _():
            xb_ref[...] = x_ref[...].astype(BF16)

        a = _dot(xb_ref[...], wg_ref[0])
        o_ref[...] = (a * _sigmoid(a) * _dot(xb_ref[...], wu_ref[0])).astype(BF16)

    @pl.when(t >= nu_ref[0])
    def _():
        o_ref[...] = jnp.zeros_like(o_ref)


def moe_up(xs, wg, wu, tile_expert, n_used):
    S, D = xs.shape
    F = wg.shape[2]
    tm = MOE_TILE
    tf = _pick(F, (512, 256, 128))
    nf = F // tf
    fidx = lambda t, f, te, nu: jnp.where(t < nu[0], f, nf - 1)
    return pl.pallas_call(
        _moe_up_kernel,
        grid_spec=pltpu.PrefetchScalarGridSpec(
            num_scalar_prefetch=2,
            grid=(S // tm, nf),
            in_specs=[
                pl.BlockSpec((tm, D), lambda t, f, te, nu: (t, 0)),
                pl.BlockSpec((1, D, tf), lambda t, f, te, nu: (te[t], 0, fidx(t, f, te, nu))),
                pl.BlockSpec((1, D, tf), lambda t, f, te, nu: (te[t], 0, fidx(t, f, te, nu))),
            ],
            out_specs=pl.BlockSpec((tm, tf), lambda t, f, te, nu: (t, f)),
            scratch_shapes=[pltpu.VMEM((tm, D), BF16)],
        ),
        out_shape=jax.ShapeDtypeStruct((S, F), BF16),
        compiler_params=_cparams("arbitrary", "arbitrary"),
    )(tile_expert, n_used, xs, wg, wu)


def _moe_down_kernel(te_ref, nu_ref, a_ref, w_ref, o_ref):
    @pl.when(pl.program_id(0) < nu_ref[0])
    def _():
        o_ref[...] = _dot(a_ref[...], w_ref[0])

    @pl.when(pl.program_id(0) >= nu_ref[0])
    def _():
        o_ref[...] = jnp.zeros_like(o_ref)


def moe_down(act, wd, tile_expert, n_used):
    S, F = act.shape
    D = wd.shape[2]
    tm = MOE_TILE
    tn = _pick(D, (512, 256, 128))
    nn = D // tn
    jidx = lambda t, j, te, nu: jnp.where(t < nu[0], j, nn - 1)
    return pl.pallas_call(
        _moe_down_kernel,
        grid_spec=pltpu.PrefetchScalarGridSpec(
            num_scalar_prefetch=2,
            grid=(S // tm, nn),
            in_specs=[
                pl.BlockSpec((tm, F), lambda t, j, te, nu: (t, 0)),
                pl.BlockSpec((1, F, tn), lambda t, j, te, nu: (te[t], 0, jidx(t, j, te, nu))),
            ],
            out_specs=pl.BlockSpec((tm, tn), lambda t, j, te, nu: (t, j)),
        ),
        out_shape=jax.ShapeDtypeStruct((S, D), F32),
        compiler_params=_cparams("arbitrary", "arbitrary"),
    )(tile_expert, n_used, act, wd)


def _combine_kernel(p1_ref, p2_ref, y_ref, gw_ref, res_ref, o_ref, y1_ref, y2_ref, sem):
    n = GATHER_ROWS

    def start(r, _):
        _row_copy(y_ref, y1_ref, sem.at[0], p1_ref[0, 0, r], r).start()
        _row_copy(y_ref, y2_ref, sem.at[1], p2_ref[0, 0, r], r).start()
        return 0

    lax.fori_loop(0, n, start, 0)

    def wait(r, _):
        _row_copy(y_ref, y1_ref, sem.at[0], 0, r).wait()
        _row_copy(y_ref, y2_ref, sem.at[1], 0, r).wait()
        return 0

    lax.fori_loop(0, n, wait, 0)
    gw = gw_ref[...]
    o_ref[...] = res_ref[...] + gw[:, 0:1] * y1_ref[...] + gw[:, 1:2] * y2_ref[...]


def moe_combine(y, pos1, pos2, gw, res):
    R, D = res.shape
    n = GATHER_ROWS
    NL = gw.shape[1]
    smem = lambda: pl.BlockSpec((1, 1, n), lambda i: (i, 0, 0), memory_space=pltpu.SMEM)
    return pl.pallas_call(
        _combine_kernel,
        grid=(R // n,),
        in_specs=[
            smem(), smem(),
            pl.BlockSpec(memory_space=pl.ANY),
            pl.BlockSpec((n, NL), lambda i: (i, 0)),
            pl.BlockSpec((n, D), lambda i: (i, 0)),
        ],
        out_specs=pl.BlockSpec((n, D), lambda i: (i, 0)),
        out_shape=jax.ShapeDtypeStruct((R, D), F32),
        scratch_shapes=[pltpu.VMEM((n, D), F32), pltpu.VMEM((n, D), F32),
                        pltpu.SemaphoreType.DMA((2,))],
        compiler_params=_cparams("arbitrary"),
    )(pos1.reshape(R // n, 1, n), pos2.reshape(R // n, 1, n), y, gw, res)


def moe_dispatch(idx, n_exp):
    R = idx.shape[0]
    tm = MOE_TILE
    A = 2 * R
    n_tiles = -(-A // tm) + n_exp
    e_flat = jnp.concatenate([idx[:, 0], idx[:, 1]])
    tok = jnp.concatenate([jnp.arange(R, dtype=jnp.int32)] * 2)
    onehot = (e_flat[:, None] == jnp.arange(n_exp, dtype=jnp.int32)[None, :]).astype(jnp.int32)
    csum = jnp.cumsum(onehot, axis=0)
    rank = jnp.sum((csum - onehot) * onehot, axis=1)
    counts = csum[-1]
    ptiles = (counts + tm - 1) // tm
    tile_end = jnp.cumsum(ptiles)
    tile_off = tile_end - ptiles
    pos = (tile_off[e_flat] * tm + rank).astype(jnp.int32)
    slot_token = jnp.zeros((n_tiles * tm,), jnp.int32).at[pos].set(tok)
    n_used = tile_end[-1].astype(jnp.int32)
    t = jnp.arange(n_tiles, dtype=jnp.int32)
    tile_expert = jnp.sum((t[:, None] >= tile_end[None, :]).astype(jnp.int32), axis=1)
    last = jnp.sum((n_used - 1 >= tile_end).astype(jnp.int32))
    tile_expert = jnp.where(t < n_used, tile_expert, last).astype(jnp.int32)
    return slot_token, pos[:R], pos[R:], tile_expert, n_used.reshape(1)


def moe_layer(h_res, g, wr_pad, wg, wu, wd, n_exp):
    xn, idx, gw = router(h_res, g, wr_pad, n_exp)
    slot_token, pos1, pos2, tile_expert, n_used = moe_dispatch(idx[:, :2], n_exp)
    xs = gather_rows(xn, slot_token)
    act = moe_up(xs, wg, wu, tile_expert, n_used)
    y = moe_down(act, wd, tile_expert, n_used)
    return moe_combine(y, pos1, pos2, gw, h_res)


def _final_kernel(x_ref, g_ref, o_ref):
    o_ref[0] = _rms(x_ref[0], g_ref[...])


def final_norm(h3, g, seq):
    B, Lp, D = h3.shape
    tm = _pick(seq, (512, 256, 128))
    return pl.pallas_call(
        _final_kernel,
        grid=(B, seq // tm),
        in_specs=[pl.BlockSpec((1, tm, D), lambda b, i: (b, i, 0)),
                  pl.BlockSpec((1, D), lambda b, i: (0, 0))],
        out_specs=pl.BlockSpec((1, tm, D), lambda b, i: (b, i, 0)),
        out_shape=jax.ShapeDtypeStruct((B, seq, D), F32),
        compiler_params=_cparams("parallel", "parallel"),
    )(h3, g.reshape(1, D))


def kernel(x, meta_tokens, norm_mix_g, w_in, w_out, attn_norm_g, ssm_lambda_re, ssm_lambda_im, ssm_log_dt, ssm_b_re, ssm_b_im, ssm_c_re, ssm_c_im, ssm_d, ssm_w_glu, ssm_norm_g, conv_w_dw, conv_b_dw, conv_ln_g, conv_ln_b, conv_w_pw, conv_norm_g, norm_ffn_g, dense_w_gate, dense_w_up, dense_w_down, moe_w_router, moe_w_gate, moe_w_up, moe_w_down, final_norm_g):
    B, seq, D = x.shape
    n_meta = meta_tokens.shape[0]
    depth = w_in.shape[0]
    AW = attn_norm_g.shape[1]
    SW = ssm_d.shape[1]
    C = conv_b_dw.shape[1]
    n_exp = moe_w_router.shape[2]
    Lp = seq + TAIL_ROWS
    R = B * Lp
    assert seq % ATTN_TILE == 0 and n_meta <= TAIL_ROWS and AW % HEAD_DIM == 0
    assert conv_w_dw.shape[1] - 1 <= HALO and SW == C

    meta = jnp.broadcast_to(meta_tokens.astype(F32)[None], (B, n_meta, D))
    pad = jnp.zeros((B, TAIL_ROWS - n_meta, D), F32)
    h_res = jnp.concatenate([x.astype(F32), pad, meta], axis=1).reshape(R, D)
    local = jnp.arange(Lp, dtype=jnp.int32)
    keep = jnp.logical_or(local < seq, local >= Lp - n_meta).astype(F32)
    keep = jnp.broadcast_to(keep[None, :], (B, Lp)).reshape(R, 1)

    q_scale = HEAD_DIM ** -0.5 * math.log2(math.e)
    qkv_scale = jnp.concatenate([jnp.full((1, AW), q_scale, F32), jnp.ones((1, 2 * AW), F32)], axis=1)
    rest_scale = jnp.ones((1, SW + 2 * C), F32)
    ti = jnp.arange(ATTN_TILE, dtype=jnp.int32)
    tri = (ti[:, None] > ti[None, :]).astype(BF16)
    NL = 128
    bf = lambda a: a.astype(BF16)

    for i in range(depth):
        w_qkv = bf(w_in[i, :, :3 * AW])
        w_rest = bf(w_in[i, :, 3 * AW:])
        qkv = rms_matmul(h_res, norm_mix_g[i], w_qkv, qkv_scale, BF16)
        rest = rms_matmul(h_res, norm_mix_g[i], w_rest, rest_scale, F32)
        attn = attention(qkv.reshape(B, Lp, 3 * AW), tri, seq, n_meta)
        bd, cd, pw = s5_params(ssm_lambda_re[i], ssm_lambda_im[i], ssm_log_dt[i], ssm_b_re[i],
                               ssm_b_im[i], ssm_c_re[i], ssm_c_im[i])
        rest3 = rest.reshape(B, Lp, SW + 2 * C)
        ssm = s5_mixer(rest3, bd, cd, pw, ssm_d[i].reshape(1, SW).astype(F32), bf(ssm_w_glu[i]),
                       ssm_norm_g[i].reshape(1, SW).astype(F32))
        conv = conv_module(rest3, SW, conv_w_dw[i].astype(F32), conv_b_dw[i].reshape(1, C).astype(F32),
                           conv_ln_g[i].reshape(1, C).astype(F32), conv_ln_b[i].reshape(1, C).astype(F32),
                           bf(conv_w_pw[i]), conv_norm_g[i].reshape(1, C).astype(F32))
        h_res = out_proj(attn.reshape(R, AW), ssm.reshape(R, SW), conv.reshape(R, C),
                         attn_norm_g[i].reshape(1, AW).astype(F32), bf(w_out[i]), h_res, keep)
        j = i // 2
        if i % 2 == 0:
            act = ffn_up(h_res, norm_ffn_g[i], bf(dense_w_gate[j]), bf(dense_w_up[j]))
            h_res = ffn_down(act, bf(dense_w_down[j]), h_res)
        else:
            wr_pad = jnp.zeros((D, NL), F32).at[:, :n_exp].set(moe_w_router[j].astype(F32))
            h_res = moe_layer(h_res, norm_ffn_g[i], wr_pad, bf(moe_w_gate[j]), bf(moe_w_up[j]),
                              bf(moe_w_down[j]), n_exp)

    return final_norm(h_res.reshape(B, Lp, D), final_norm_g.astype(F32), seq).astype(x.dtype)
```

```python
import functools
import math

import jax
import jax.numpy as jnp
from jax import lax
from jax.experimental import pallas as pl
from jax.experimental.pallas import tpu as pltpu

F32 = jnp.float32
BF16 = jnp.bfloat16
EPS = 1e-6

HEAD_DIM = 128
SSM_GROUP_CH = 16
TAIL_ROWS = 128
ATTN_TILE = 256
ATTN_ROW_TILES = 2
ATTN_HEADS_PER_STEP = 4
SEQ_TILE = 128
SUBLANES = 8
HALO = 32
VMEM_LIMIT = 56 * 1024 * 1024
MOE_TILE = 512
GATHER_ROWS = 256


def _pick(n, prefs):
    for p in prefs:
        if n % p == 0:
            return p
    raise ValueError(f"no tile for {n} in {prefs}")


def _cparams(*sem):
    return pltpu.CompilerParams(dimension_semantics=sem, vmem_limit_bytes=VMEM_LIMIT)


def _dot(a, b):
    return jnp.dot(a, b, preferred_element_type=F32)


def _rms(x, g):
    ms = jnp.mean(x * x, axis=-1, keepdims=True)
    return x * lax.rsqrt(ms + EPS) * g


def _sigmoid(x):
    return 1.0 / (1.0 + jnp.exp(-x))


def _rms_matmul_kernel(x_ref, g_ref, w_ref, s_ref, o_ref, xn_ref):
    @pl.when(pl.program_id(1) == 0)
    def _():
        xn_ref[...] = _rms(x_ref[...], g_ref[...]).astype(BF16)

    o_ref[...] = (_dot(xn_ref[...], w_ref[...]) * s_ref[...]).astype(o_ref.dtype)


def rms_matmul(x, g, w, col_scale, out_dtype):
    R, D = x.shape
    N = w.shape[1]
    tm = _pick(R, (768, 512, 256, 128))
    tn = _pick(N, (768, 512, 384, 256, 128))
    return pl.pallas_call(
        _rms_matmul_kernel,
        grid=(R // tm, N // tn),
        in_specs=[
            pl.BlockSpec((tm, D), lambda i, j: (i, 0)),
            pl.BlockSpec((1, D), lambda i, j: (0, 0)),
            pl.BlockSpec((D, tn), lambda i, j: (0, j)),
            pl.BlockSpec((1, tn), lambda i, j: (0, j)),
        ],
        out_specs=pl.BlockSpec((tm, tn), lambda i, j: (i, j)),
        out_shape=jax.ShapeDtypeStruct((R, N), out_dtype),
        scratch_shapes=[pltpu.VMEM((tm, D), BF16)],
        compiler_params=_cparams("parallel", "arbitrary"),
        name="rms_in_proj",
    )(x, g.reshape(1, D), w, col_scale)


def _sb_block(q, kb, vb, later, car, mask):
    z = lax.dot_general(q, kb, (((1,), (1,)), ((), ())), preferred_element_type=F32)
    neg_abs = lax.bitcast_convert_type(
        lax.bitcast_convert_type(z, jnp.uint32) | jnp.uint32(0x80000000), F32)
    lb = jnp.minimum(z, 0.0) - jnp.log2(1.0 + jnp.exp2(neg_abs))
    l1m = lb - z
    if mask is not None:
        l1m = jnp.where(mask, l1m, 0.0)
    after = _dot(l1m.astype(BF16), later)
    w = jnp.exp2(lb + after + car)
    if mask is not None:
        w = jnp.where(mask, w, 0.0)
    return _dot(w.astype(BF16), vb), jnp.sum(l1m, axis=-1, keepdims=True)


def _attn_kernel(q_ref, k_ref, v_ref, tri_ref, trit_ref, o_ref, acc_ref, car_ref,
                 *, seq, n_meta, m, hp):
    i = pl.program_id(2)
    tk = ATTN_TILE
    tq = m * tk
    tb = TAIL_ROWS
    dh = HEAD_DIM
    n_real = seq // tq
    first_meta = tb - n_meta

    def step(r0, nrows, koff, nk, tri_ext, mask):
        for h in range(hp):
            cs = slice(h * dh, (h + 1) * dh)
            pv, rs = _sb_block(q_ref[0, r0:r0 + nrows, cs], k_ref[0, pl.ds(koff, nk), cs],
                               v_ref[0, pl.ds(koff, nk), cs], tri_ext,
                               car_ref[h, r0:r0 + nrows, :], mask)
            acc_ref[h, r0:r0 + nrows, :] += pv
            car_ref[h, r0:r0 + nrows, :] += rs

    acc_ref[...] = jnp.zeros_like(acc_ref)
    car_ref[...] = jnp.zeros_like(car_ref)

    @pl.when(i < n_real)
    def _():
        for d in reversed(range(m)):
            nrows = tq - d * tk
            rr = lax.broadcasted_iota(jnp.int32, (nrows, tk), 0)
            cc = lax.broadcasted_iota(jnp.int32, (nrows, tk), 1)
            step(d * tk, nrows, pl.multiple_of((i * m + d) * tk, tk), tk, tri_ref[...], cc < rr)

        def body(n, _):
            step(0, tq, pl.multiple_of((i * m - 1 - n) * tk, tk), tk, tri_ref[...], None)
            return 0

        lax.fori_loop(0, i * m, body, 0)
        ct = lax.broadcasted_iota(jnp.int32, (tq, tb), 1)
        step(0, tq, seq, tb, trit_ref[...], ct >= first_meta)
        for h in range(hp):
            o_ref[0, :, h * dh:(h + 1) * dh] = acc_ref[h]

    @pl.when(i == n_real)
    def _():
        rr = lax.broadcasted_iota(jnp.int32, (tb, tb), 0)
        cc = lax.broadcasted_iota(jnp.int32, (tb, tb), 1)
        step(0, tb, seq, tb, trit_ref[...], jnp.logical_and(cc < rr, cc >= first_meta))
        o_ref[...] = jnp.zeros_like(o_ref)
        for h in range(hp):
            o_ref[0, 0:tb, h * dh:(h + 1) * dh] = acc_ref[h, 0:tb, :]


def attention(qkv, seq, n_meta):
    B, Lp, W3 = qkv.shape
    H = W3 // (3 * HEAD_DIM)
    tk = ATTN_TILE
    m = ATTN_ROW_TILES if seq % (ATTN_ROW_TILES * tk) == 0 else 1
    hp = ATTN_HEADS_PER_STEP if H % ATTN_HEADS_PER_STEP == 0 else 1
    tq = m * tk
    wb = hp * HEAD_DIM
    ng = H // hp
    ti = jnp.arange(tk, dtype=jnp.int32)
    tri = (ti[:, None] > ti[None, :]).astype(BF16)
    trit = tri[:TAIL_ROWS, :TAIL_ROWS]
    return pl.pallas_call(
        functools.partial(_attn_kernel, seq=seq, n_meta=n_meta, m=m, hp=hp),
        grid=(B, ng, seq // tq + 1),
        in_specs=[
            pl.BlockSpec((1, tq, wb), lambda b, g, i: (b, i, g)),
            pl.BlockSpec((1, Lp, wb), lambda b, g, i: (b, 0, ng + g)),
            pl.BlockSpec((1, Lp, wb), lambda b, g, i: (b, 0, 2 * ng + g)),
            pl.BlockSpec(tri.shape, lambda b, g, i: (0, 0)),
            pl.BlockSpec(trit.shape, lambda b, g, i: (0, 0)),
        ],
        out_specs=pl.BlockSpec((1, tq, wb), lambda b, g, i: (b, i, g)),
        out_shape=jax.ShapeDtypeStruct((B, Lp, H * HEAD_DIM), F32),
        scratch_shapes=[pltpu.VMEM((hp, tq, HEAD_DIM), F32), pltpu.VMEM((hp, tq, 1), F32)],
        compiler_params=_cparams("parallel", "parallel", "arbitrary"),
        name="sb_attention",
    )(qkv, qkv, qkv, tri, trit)


def _gelu_tanh(x):
    return 0.5 * x * (1.0 + jnp.tanh(math.sqrt(2.0 / math.pi) * (x + 0.044715 * (x * x * x))))


def _s5_kernel(u_ref, bd_ref, cd_ref, pw_ref, d_ref, wglu_ref, g_ref, o_ref,
               carry_ref, st_ref, y_ref, *, nh, cw, cs):
    tc = SEQ_TILE

    @pl.when(pl.program_id(1) == 0)
    def _():
        carry_ref[...] = jnp.zeros_like(carry_ref)

    u = u_ref[0]
    ub = u.astype(BF16)
    for c in range(nh):
        st_ref[...] = _dot(ub[:, c * cw:(c + 1) * cw], bd_ref[c])

        def tile(r, carry, c=c):
            cre, cim = carry
            off = pl.multiple_of(r * SUBLANES, SUBLANES)
            xre = st_ref[pl.ds(off, SUBLANES), 0:cs]
            xim = st_ref[pl.ds(off, SUBLANES), cs:2 * cs]
            for kk, sh in enumerate((1, 2, 4)):
                pr = pw_ref[c, kk, :, 0:cs]
                pi = pw_ref[c, kk, :, cs:2 * cs]
                sre = pltpu.roll(xre, sh, axis=0)
                sim = pltpu.roll(xim, sh, axis=0)
                xre, xim = xre + pr * sre - pi * sim, xim + pr * sim + pi * sre
            pr = pw_ref[c, 3, :, 0:cs]
            pi = pw_ref[c, 3, :, cs:2 * cs]
            xre, xim = xre + pr * cre - pi * cim, xim + pr * cim + pi * cre
            st_ref[pl.ds(off, SUBLANES), 0:cs] = xre
            st_ref[pl.ds(off, SUBLANES), cs:2 * cs] = xim
            return (jnp.broadcast_to(xre[SUBLANES - 1:SUBLANES, :], (SUBLANES, cs)),
                    jnp.broadcast_to(xim[SUBLANES - 1:SUBLANES, :], (SUBLANES, cs)))

        cre, cim = lax.fori_loop(0, tc // SUBLANES, tile,
                                 (carry_ref[c, :, 0:cs], carry_ref[c, :, cs:2 * cs]))
        carry_ref[c, :, 0:cs] = cre
        carry_ref[c, :, cs:2 * cs] = cim
        y_ref[:, c * cw:(c + 1) * cw] = _dot(st_ref[...].astype(BF16), cd_ref[c])

    y = y_ref[...] + d_ref[...] * u
    g = _gelu_tanh(y)
    out = g * _sigmoid(_dot(g.astype(BF16), wglu_ref[...]))
    o_ref[0] = _rms(out, g_ref[...]).astype(BF16)


def s5_mixer(rest, bd, cd, pw, d, wglu, g):
    B, Lp, _ = rest.shape
    nh, cw, cs2 = bd.shape
    cs = cs2 // 2
    SW = nh * cw
    tc = SEQ_TILE
    nblk = Lp // tc
    seq_blk = lambda b, s: (b, (s + nblk - 1) % nblk, 0)
    full = lambda *shape: pl.BlockSpec(shape, lambda b, s: (0,) * len(shape))
    return pl.pallas_call(
        functools.partial(_s5_kernel, nh=nh, cw=cw, cs=cs),
        grid=(B, nblk),
        in_specs=[
            pl.BlockSpec((1, tc, SW), seq_blk),
            full(nh, cw, 2 * cs), full(nh, 2 * cs, cw), full(nh, 4, SUBLANES, 2 * cs),
            full(1, SW), full(SW, SW), full(1, SW),
        ],
        out_specs=pl.BlockSpec((1, tc, SW), seq_blk),
        out_shape=jax.ShapeDtypeStruct((B, Lp, SW), BF16),
        scratch_shapes=[pltpu.VMEM((nh, SUBLANES, 2 * cs), F32),
                        pltpu.VMEM((tc, 2 * cs), F32),
                        pltpu.VMEM((tc, SW), F32)],
        compiler_params=_cparams("parallel", "arbitrary"),
        name="s5_mixer",
    )(rest, bd, cd, pw, d, wglu, g)


def s5_params(lam_re, lam_im, log_dt, b_re, b_im, c_re, c_im):
    G, P = lam_re.shape
    Hc = SSM_GROUP_CH
    dt = jnp.exp(log_dt.astype(F32))[:, None]
    lr = lam_re.astype(F32)
    li = lam_im.astype(F32)
    mag = jnp.exp(lr * dt)
    ang = li * dt
    ab_re = mag * jnp.cos(ang)
    ab_im = mag * jnp.sin(ang)
    den = lr * lr + li * li
    nr = ab_re - 1.0
    ni = ab_im
    coef_re = (nr * lr + ni * li) / den
    coef_im = (ni * lr - nr * li) / den
    br = b_re.astype(F32)
    bi = b_im.astype(F32)
    bbar_re = coef_re[..., None] * br - coef_im[..., None] * bi
    bbar_im = coef_re[..., None] * bi + coef_im[..., None] * br

    SW = G * Hc
    cw = min(SW, 256)
    gc = cw // Hc
    nh = G // gc
    cs = gc * P
    eye = jnp.eye(gc, dtype=F32)

    def in_blockdiag(m):
        m = m.reshape(nh, gc, P, Hc)
        return jnp.einsum('ngph,gk->nghkp', m, eye).reshape(nh, gc * Hc, gc * P)

    def out_blockdiag(m):
        m = m.reshape(nh, gc, Hc, P)
        return jnp.einsum('nghp,gk->ngpkh', m, eye).reshape(nh, gc * P, gc * Hc)

    bd = jnp.concatenate([in_blockdiag(bbar_re), in_blockdiag(bbar_im)], axis=-1).astype(BF16)
    cd = jnp.concatenate([out_blockdiag(c_re.astype(F32)),
                          -out_blockdiag(c_im.astype(F32))], axis=1).astype(BF16)

    rows = jnp.arange(SUBLANES, dtype=F32)[:, None, None]

    def power(e):
        m = jnp.exp(e * (lr * dt)[None])
        a = e * ang[None]
        return m * jnp.cos(a), m * jnp.sin(a)

    tabs = []
    for sh in (1, 2, 4):
        pr, pi = power(jnp.full((SUBLANES, 1, 1), float(sh), F32))
        keep = rows >= sh
        tabs.append((jnp.where(keep, pr, 0.0), jnp.where(keep, pi, 0.0)))
    tabs.append(power(rows + 1.0))
    pw = jnp.stack([jnp.concatenate([pr.reshape(SUBLANES, nh, cs), pi.reshape(SUBLANES, nh, cs)],
                                    axis=-1) for pr, pi in tabs], axis=0)
    pw = pw.transpose(2, 0, 1, 3)
    return bd, cd, pw


def _conv_kernel(val_ref, gate_ref, hval_ref, hgate_ref, wdw_ref, bdw_ref, lng_ref, lnb_ref,
                 wpw_ref, ng_ref, o_ref, hbuf_ref, *, taps, nblk):
    tc = SEQ_TILE
    halo = hval_ref[0] * _sigmoid(hgate_ref[0])
    halo = jnp.where(pl.program_id(1) == nblk - 1, 0.0, halo)
    hbuf_ref[0:HALO, :] = halo
    hbuf_ref[HALO:HALO + tc, :] = val_ref[0] * _sigmoid(gate_ref[0])
    acc = jnp.broadcast_to(bdw_ref[...], o_ref.shape[1:]).astype(F32)
    base = HALO - (taps - 1)
    for k in range(taps):
        acc = acc + wdw_ref[k:k + 1, :] * hbuf_ref[base + k:base + k + tc, :]
    mu = jnp.mean(acc, axis=-1, keepdims=True)
    xc = acc - mu
    var = jnp.mean(xc * xc, axis=-1, keepdims=True)
    y = xc * lax.rsqrt(var + EPS) * lng_ref[...] + lnb_ref[...]
    y = y * _sigmoid(y)
    out = _dot(y.astype(BF16), wpw_ref[...])
    o_ref[0] = _rms(out, ng_ref[...]).astype(BF16)


def conv_module(rest, col0, wdw, bdw, lng, lnb, wpw, ng):
    B, Lp, _ = rest.shape
    taps, C = wdw.shape
    tc = SEQ_TILE
    nblk = Lp // tc
    vb = col0 // C
    per = tc // HALO
    prev_tail = lambda b, p: ((p + nblk - 1) % nblk) * per + per - 1
    full = lambda *shape: pl.BlockSpec(shape, lambda b, p: (0,) * len(shape))
    return pl.pallas_call(
        functools.partial(_conv_kernel, taps=taps, nblk=nblk),
        grid=(B, nblk),
        in_specs=[
            pl.BlockSpec((1, tc, C), lambda b, p: (b, p, vb)),
            pl.BlockSpec((1, tc, C), lambda b, p: (b, p, vb + 1)),
            pl.BlockSpec((1, HALO, C), lambda b, p: (b, prev_tail(b, p), vb)),
            pl.BlockSpec((1, HALO, C), lambda b, p: (b, prev_tail(b, p), vb + 1)),
            full(taps, C), full(1, C), full(1, C), full(1, C), full(C, C), full(1, C),
        ],
        out_specs=pl.BlockSpec((1, tc, C), lambda b, p: (b, p, 0)),
        out_shape=jax.ShapeDtypeStruct((B, Lp, C), BF16),
        scratch_shapes=[pltpu.VMEM((HALO + tc, C), F32)],
        compiler_params=_cparams("parallel", "parallel"),
        name="conv_module",
    )(rest, rest, rest, rest, wdw, bdw, lng, lnb, wpw, ng)


def _outproj_kernel(attn_ref, ssm_ref, conv_ref, ag_ref, w_ref, res_ref, keep_ref, o_ref,
                    an_ref, *, aw, sw):
    @pl.when(pl.program_id(1) == 0)
    def _():
        an_ref[...] = _rms(attn_ref[...], ag_ref[...]).astype(BF16)

    acc = (_dot(an_ref[...], w_ref[0:aw, :]) + _dot(ssm_ref[...], w_ref[aw:aw + sw, :])
           + _dot(conv_ref[...], w_ref[aw + sw:, :]))
    o_ref[...] = res_ref[...] + jnp.where(keep_ref[...] > 0.0, acc, 0.0)


def out_proj(attn, ssm, conv, ag, w, res, keep):
    R, AW = attn.shape
    SW = ssm.shape[1]
    C = conv.shape[1]
    D = w.shape[1]
    tm = _pick(R, (768, 512, 256, 128))
    tn = _pick(D, (1024, 512, 256, 128))
    return pl.pallas_call(
        functools.partial(_outproj_kernel, aw=AW, sw=SW),
        grid=(R // tm, D // tn),
        in_specs=[
            pl.BlockSpec((tm, AW), lambda i, j: (i, 0)),
            pl.BlockSpec((tm, SW), lambda i, j: (i, 0)),
            pl.BlockSpec((tm, C), lambda i, j: (i, 0)),
            pl.BlockSpec((1, AW), lambda i, j: (0, 0)),
            pl.BlockSpec((AW + SW + C, tn), lambda i, j: (0, j)),
            pl.BlockSpec((tm, tn), lambda i, j: (i, j)),
            pl.BlockSpec((tm, 1), lambda i, j: (i, 0)),
        ],
        out_specs=pl.BlockSpec((tm, tn), lambda i, j: (i, j)),
        out_shape=jax.ShapeDtypeStruct((R, D), F32),
        scratch_shapes=[pltpu.VMEM((tm, AW), BF16)],
        compiler_params=_cparams("parallel", "arbitrary"),
        name="out_proj",
    )(attn, ssm, conv, ag, w, res, keep)


def _ffn_up_kernel(x_ref, g_ref, wg_ref, wu_ref, o_ref, xn_ref):
    @pl.when(pl.program_id(1) == 0)
    def _():
        xn_ref[...] = _rms(x_ref[...], g_ref[...]).astype(BF16)

    a = _dot(xn_ref[...], wg_ref[...])
    o_ref[...] = (a * _sigmoid(a) * _dot(xn_ref[...], wu_ref[...])).astype(BF16)


def ffn_up(x, g, wg, wu):
    R, D = x.shape
    F = wg.shape[1]
    tm = _pick(R, (768, 512, 256, 128))
    tf = _pick(F, (512, 256, 128))
    return pl.pallas_call(
        _ffn_up_kernel,
        grid=(R // tm, F // tf),
        in_specs=[
            pl.BlockSpec((tm, D), lambda i, j: (i, 0)),
            pl.BlockSpec((1, D), lambda i, j: (0, 0)),
            pl.BlockSpec((D, tf), lambda i, j: (0, j)),
            pl.BlockSpec((D, tf), lambda i, j: (0, j)),
        ],
        out_specs=pl.BlockSpec((tm, tf), lambda i, j: (i, j)),
        out_shape=jax.ShapeDtypeStruct((R, F), BF16),
        scratch_shapes=[pltpu.VMEM((tm, D), BF16)],
        compiler_params=_cparams("parallel", "arbitrary"),
        name="ffn_up",
    )(x, g.reshape(1, D), wg, wu)


def _ffn_down_kernel(a_ref, w_ref, res_ref, o_ref):
    o_ref[...] = res_ref[...] + _dot(a_ref[...], w_ref[...])


def ffn_down(act, w, res):
    R, F = act.shape
    D = w.shape[1]
    tm = _pick(R, (768, 512, 256, 128))
    tn = _pick(D, (512, 256, 128))
    return pl.pallas_call(
        _ffn_down_kernel,
        grid=(R // tm, D // tn),
        in_specs=[
            pl.BlockSpec((tm, F), lambda i, j: (i, 0)),
            pl.BlockSpec((F, tn), lambda i, j: (0, j)),
            pl.BlockSpec((tm, tn), lambda i, j: (i, j)),
        ],
        out_specs=pl.BlockSpec((tm, tn), lambda i, j: (i, j)),
        out_shape=jax.ShapeDtypeStruct((R, D), F32),
        compiler_params=_cparams("parallel", "parallel"),
        name="ffn_down",
    )(act, w, res)


def _router_kernel(x_ref, g_ref, wr_ref, xn_ref, idx_ref, gw_ref, *, n_exp):
    xn = _rms(x_ref[...], g_ref[...])
    xn_ref[...] = xn
    logits = jnp.dot(xn, wr_ref[...], preferred_element_type=F32,
                     precision=lax.Precision.HIGHEST)
    lane = lax.broadcasted_iota(jnp.int32, logits.shape, 1)
    lanef = lane.astype(F32)
    big = float(logits.shape[1])
    logits = jnp.where(lane < n_exp, logits, -jnp.inf)
    m1 = jnp.max(logits, axis=-1, keepdims=True)
    i1 = jnp.min(jnp.where(logits == m1, lanef, big), axis=-1, keepdims=True)
    rest = jnp.where(lanef == i1, -jnp.inf, logits)
    m2 = jnp.max(rest, axis=-1, keepdims=True)
    i2 = jnp.min(jnp.where(rest == m2, lanef, big), axis=-1, keepdims=True)
    e = jnp.exp(m2 - m1)
    w1 = 1.0 / (1.0 + e)
    w2 = e / (1.0 + e)
    idx_ref[...] = jnp.where(lane == 0, i1, jnp.where(lane == 1, i2, 0.0)).astype(jnp.int32)
    gw_ref[...] = jnp.where(lane == 0, w1, jnp.where(lane == 1, w2, 0.0))


def router(x, g, wr_pad, n_exp):
    R, D = x.shape
    tm = _pick(R, (512, 256, 128))
    NL = wr_pad.shape[1]
    return pl.pallas_call(
        functools.partial(_router_kernel, n_exp=n_exp),
        grid=(R // tm,),
        in_specs=[
            pl.BlockSpec((tm, D), lambda i: (i, 0)),
            pl.BlockSpec((1, D), lambda i: (0, 0)),
            pl.BlockSpec((D, NL), lambda i: (0, 0)),
        ],
        out_specs=[
            pl.BlockSpec((tm, D), lambda i: (i, 0)),
            pl.BlockSpec((tm, NL), lambda i: (i, 0)),
            pl.BlockSpec((tm, NL), lambda i: (i, 0)),
        ],
        out_shape=[jax.ShapeDtypeStruct((R, D), F32),
                   jax.ShapeDtypeStruct((R, NL), jnp.int32),
                   jax.ShapeDtypeStruct((R, NL), F32)],
        compiler_params=_cparams("parallel"),
        name="moe_router",
    )(x, g.reshape(1, D), wr_pad)


def _row_copy(src_ref, dst_ref, sem, s, d):
    return pltpu.make_async_copy(src_ref.at[pl.ds(s, 1), :], dst_ref.at[pl.ds(d, 1), :], sem)


def _moe_up_kernel(te_ref, nu_ref, idx0_ref, idxn_ref, xn_ref, wg_ref, wu_ref, o_ref,
                   xf_ref, xb_ref, sem):
    t = pl.program_id(0)
    tm = MOE_TILE
    n_used = nu_ref[0]

    def start_gather(idx_ref, slot):
        def body(r, _):
            pltpu.make_async_copy(xn_ref.at[pl.ds(idx_ref[0, 0, r], 1), :],
                                  xf_ref.at[slot, pl.ds(r, 1), :], sem.at[slot]).start()
            return 0

        lax.fori_loop(0, tm, body, 0)

    @pl.when(pl.program_id(1) == 0)
    def _():
        slot = t % 2

        @pl.when(t == 0)
        def _():
            start_gather(idx0_ref, 0)

        @pl.when(t < n_used)
        def _():
            pltpu.make_async_copy(xf_ref.at[slot], xf_ref.at[slot], sem.at[slot]).wait()
            xb_ref[...] = xf_ref[slot].astype(BF16)

        @pl.when(t + 1 < n_used)
        def _():
            start_gather(idxn_ref, 1 - slot)

    @pl.when(t < n_used)
    def _():
        a = _dot(xb_ref[...], wg_ref[0])
        o_ref[...] = (a * _sigmoid(a) * _dot(xb_ref[...], wu_ref[0])).astype(BF16)

    @pl.when(t >= n_used)
    def _():
        o_ref[...] = jnp.zeros_like(o_ref)


def moe_up(xn, slot_token, wg, wu, tile_expert, n_used):
    D = xn.shape[1]
    S = slot_token.shape[0]
    F = wg.shape[2]
    tm = MOE_TILE
    nt = S // tm
    tf = _pick(F, (512, 256, 128))
    nf = F // tf
    fidx = lambda t, f, te, nu: jnp.where(t < nu[0], f, nf - 1)
    idx3 = slot_token.reshape(nt, 1, tm)
    return pl.pallas_call(
        _moe_up_kernel,
        grid_spec=pltpu.PrefetchScalarGridSpec(
            num_scalar_prefetch=2,
            grid=(nt, nf),
            in_specs=[
                pl.BlockSpec((1, 1, tm), lambda t, f, te, nu: (0, 0, 0), memory_space=pltpu.SMEM),
                pl.BlockSpec((1, 1, tm), lambda t, f, te, nu: (jnp.minimum(t + 1, nt - 1), 0, 0),
                             memory_space=pltpu.SMEM),
                pl.BlockSpec(memory_space=pl.ANY),
                pl.BlockSpec((1, D, tf), lambda t, f, te, nu: (te[t], 0, fidx(t, f, te, nu))),
                pl.BlockSpec((1, D, tf), lambda t, f, te, nu: (te[t], 0, fidx(t, f, te, nu))),
            ],
            out_specs=pl.BlockSpec((tm, tf), lambda t, f, te, nu: (t, f)),
            scratch_shapes=[pltpu.VMEM((2, tm, D), F32), pltpu.VMEM((tm, D), BF16),
                            pltpu.SemaphoreType.DMA((2,))],
        ),
        out_shape=jax.ShapeDtypeStruct((S, F), BF16),
        compiler_params=_cparams("arbitrary", "arbitrary"),
        name="moe_gather_up",
    )(tile_expert, n_used, idx3, idx3, xn, wg, wu)


def _moe_down_kernel(te_ref, nu_ref, a_ref, w_ref, o_ref):
    @pl.when(pl.program_id(0) < nu_ref[0])
    def _():
        o_ref[...] = _dot(a_ref[...], w_ref[0])

    @pl.when(pl.program_id(0) >= nu_ref[0])
    def _():
        o_ref[...] = jnp.zeros_like(o_ref)


def moe_down(act, wd, tile_expert, n_used):
    S, F = act.shape
    D = wd.shape[2]
    tm = MOE_TILE
    tn = _pick(D, (512, 256, 128))
    nn = D // tn
    jidx = lambda t, j, te, nu: jnp.where(t < nu[0], j, nn - 1)
    return pl.pallas_call(
        _moe_down_kernel,
        grid_spec=pltpu.PrefetchScalarGridSpec(
            num_scalar_prefetch=2,
            grid=(S // tm, nn),
            in_specs=[
                pl.BlockSpec((tm, F), lambda t, j, te, nu: (t, 0)),
                pl.BlockSpec((1, F, tn), lambda t, j, te, nu: (te[t], 0, jidx(t, j, te, nu))),
            ],
            out_specs=pl.BlockSpec((tm, tn), lambda t, j, te, nu: (t, j)),
        ),
        out_shape=jax.ShapeDtypeStruct((S, D), F32),
        compiler_params=_cparams("arbitrary", "arbitrary"),
        name="moe_down",
    )(tile_expert, n_used, act, wd)


def _combine_kernel(p1_ref, p2_ref, y_ref, gw_ref, res_ref, o_ref, y1_ref, y2_ref, sem):
    n = GATHER_ROWS

    def start(r, _):
        _row_copy(y_ref, y1_ref, sem.at[0], p1_ref[0, 0, r], r).start()
        _row_copy(y_ref, y2_ref, sem.at[1], p2_ref[0, 0, r], r).start()
        return 0

    lax.fori_loop(0, n, start, 0)
    pltpu.make_async_copy(y1_ref, y1_ref, sem.at[0]).wait()
    pltpu.make_async_copy(y2_ref, y2_ref, sem.at[1]).wait()
    gw = gw_ref[...]
    o_ref[...] = res_ref[...] + gw[:, 0:1] * y1_ref[...] + gw[:, 1:2] * y2_ref[...]


def moe_combine(y, pos1, pos2, gw, res):
    R, D = res.shape
    n = GATHER_ROWS
    NL = gw.shape[1]
    smem = lambda: pl.BlockSpec((1, 1, n), lambda i: (i, 0, 0), memory_space=pltpu.SMEM)
    return pl.pallas_call(
        _combine_kernel,
        grid=(R // n,),
        in_specs=[
            smem(), smem(),
            pl.BlockSpec(memory_space=pl.ANY),
            pl.BlockSpec((n, NL), lambda i: (i, 0)),
            pl.BlockSpec((n, D), lambda i: (i, 0)),
        ],
        out_specs=pl.BlockSpec((n, D), lambda i: (i, 0)),
        out_shape=jax.ShapeDtypeStruct((R, D), F32),
        scratch_shapes=[pltpu.VMEM((n, D), F32), pltpu.VMEM((n, D), F32),
                        pltpu.SemaphoreType.DMA((2,))],
        compiler_params=_cparams("arbitrary"),
        name="moe_combine",
    )(pos1.reshape(R // n, 1, n), pos2.reshape(R // n, 1, n), y, gw, res)


def moe_dispatch(idx, n_exp):
    R = idx.shape[0]
    tm = MOE_TILE
    A = 2 * R
    n_tiles = -(-A // tm) + n_exp
    e_flat = jnp.concatenate([idx[:, 0], idx[:, 1]])
    tok = jnp.concatenate([jnp.arange(R, dtype=jnp.int32)] * 2)
    onehot = (e_flat[:, None] == jnp.arange(n_exp, dtype=jnp.int32)[None, :]).astype(jnp.int32)
    csum = jnp.cumsum(onehot, axis=0)
    rank = jnp.sum((csum - onehot) * onehot, axis=1)
    counts = csum[-1]
    ptiles = (counts + tm - 1) // tm
    tile_end = jnp.cumsum(ptiles)
    tile_off = tile_end - ptiles
    pos = (tile_off[e_flat] * tm + rank).astype(jnp.int32)
    slot_token = jnp.zeros((n_tiles * tm,), jnp.int32).at[pos].set(tok)
    n_used = tile_end[-1].astype(jnp.int32)
    t = jnp.arange(n_tiles, dtype=jnp.int32)
    tile_expert = jnp.sum((t[:, None] >= tile_end[None, :]).astype(jnp.int32), axis=1)
    last = jnp.sum((n_used - 1 >= tile_end).astype(jnp.int32))
    tile_expert = jnp.where(t < n_used, tile_expert, last).astype(jnp.int32)
    return slot_token, pos[:R], pos[R:], tile_expert, n_used.reshape(1)


def moe_layer(h_res, g, wr_pad, wg, wu, wd, n_exp):
    xn, idx, gw = router(h_res, g, wr_pad, n_exp)
    slot_token, pos1, pos2, tile_expert, n_used = moe_dispatch(idx[:, :2], n_exp)
    act = moe_up(xn, slot_token, wg, wu, tile_expert, n_used)
    y = moe_down(act, wd, tile_expert, n_used)
    return moe_combine(y, pos1, pos2, gw, h_res)


def _final_kernel(x_ref, g_ref, o_ref):
    o_ref[0] = _rms(x_ref[0], g_ref[...])


def final_norm(h3, g, seq):
    B, Lp, D = h3.shape
    tm = _pick(seq, (512, 256, 128))
    return pl.pallas_call(
        _final_kernel,
        grid=(B, seq // tm),
        in_specs=[pl.BlockSpec((1, tm, D), lambda b, i: (b, i, 0)),
                  pl.BlockSpec((1, D), lambda b, i: (0, 0))],
        out_specs=pl.BlockSpec((1, tm, D), lambda b, i: (b, i, 0)),
        out_shape=jax.ShapeDtypeStruct((B, seq, D), F32),
        compiler_params=_cparams("parallel", "parallel"),
        name="final_norm",
    )(h3, g.reshape(1, D))


def kernel(x, meta_tokens, norm_mix_g, w_in, w_out, attn_norm_g, ssm_lambda_re, ssm_lambda_im, ssm_log_dt, ssm_b_re, ssm_b_im, ssm_c_re, ssm_c_im, ssm_d, ssm_w_glu, ssm_norm_g, conv_w_dw, conv_b_dw, conv_ln_g, conv_ln_b, conv_w_pw, conv_norm_g, norm_ffn_g, dense_w_gate, dense_w_up, dense_w_down, moe_w_router, moe_w_gate, moe_w_up, moe_w_down, final_norm_g):
    B, seq, D = x.shape
    n_meta = meta_tokens.shape[0]
    depth = w_in.shape[0]
    AW = attn_norm_g.shape[1]
    SW = ssm_d.shape[1]
    C = conv_b_dw.shape[1]
    n_exp = moe_w_router.shape[2]
    Lp = seq + TAIL_ROWS
    R = B * Lp
    assert seq % ATTN_TILE == 0 and n_meta <= TAIL_ROWS and AW % HEAD_DIM == 0
    assert conv_w_dw.shape[1] - 1 <= HALO and SW == C

    meta = jnp.broadcast_to(meta_tokens.astype(F32)[None], (B, n_meta, D))
    pad = jnp.zeros((B, TAIL_ROWS - n_meta, D), F32)
    h_res = jnp.concatenate([x.astype(F32), pad, meta], axis=1).reshape(R, D)
    local = jnp.arange(Lp, dtype=jnp.int32)
    keep = jnp.logical_or(local < seq, local >= Lp - n_meta).astype(F32)
    keep = jnp.broadcast_to(keep[None, :], (B, Lp)).reshape(R, 1)

    q_scale = HEAD_DIM ** -0.5 * math.log2(math.e)
    qkv_scale = jnp.concatenate([jnp.full((1, AW), q_scale, F32), jnp.ones((1, 2 * AW), F32)], axis=1)
    rest_scale = jnp.ones((1, SW + 2 * C), F32)
    NL = 128
    bf = lambda a: a.astype(BF16)

    for i in range(depth):
        w_qkv = bf(w_in[i, :, :3 * AW])
        w_rest = bf(w_in[i, :, 3 * AW:])
        qkv = rms_matmul(h_res, norm_mix_g[i], w_qkv, qkv_scale, BF16)
        rest = rms_matmul(h_res, norm_mix_g[i], w_rest, rest_scale, F32)
        attn = attention(qkv.reshape(B, Lp, 3 * AW), seq, n_meta)
        bd, cd, pw = s5_params(ssm_lambda_re[i], ssm_lambda_im[i], ssm_log_dt[i], ssm_b_re[i],
                               ssm_b_im[i], ssm_c_re[i], ssm_c_im[i])
        rest3 = rest.reshape(B, Lp, SW + 2 * C)
        ssm = s5_mixer(rest3, bd, cd, pw, ssm_d[i].reshape(1, SW).astype(F32), bf(ssm_w_glu[i]),
                       ssm_norm_g[i].reshape(1, SW).astype(F32))
        conv = conv_module(rest3, SW, conv_w_dw[i].astype(F32), conv_b_dw[i].reshape(1, C).astype(F32),
                           conv_ln_g[i].reshape(1, C).astype(F32), conv_ln_b[i].reshape(1, C).astype(F32),
                           bf(conv_w_pw[i]), conv_norm_g[i].reshape(1, C).astype(F32))
        h_res = out_proj(attn.reshape(R, AW), ssm.reshape(R, SW), conv.reshape(R, C),
                         attn_norm_g[i].reshape(1, AW).astype(F32), bf(w_out[i]), h_res, keep)
        j = i // 2
        if i % 2 == 0:
            act = ffn_up(h_res, norm_ffn_g[i], bf(dense_w_gate[j]), bf(dense_w_up[j]))
            h_res = ffn_down(act, bf(dense_w_down[j]), h_res)
        else:
            wr_pad = jnp.zeros((D, NL), F32).at[:, :n_exp].set(moe_w_router[j].astype(F32))
            h_res = moe_layer(h_res, norm_ffn_g[i], wr_pad, bf(moe_w_gate[j]), bf(moe_w_up[j]),
                              bf(moe_w_down[j]), n_exp)

    return final_norm(h_res.reshape(B, Lp, D), final_norm_g.astype(F32), seq).astype(x.dtype)
```

```python
import functools
import math

import jax
import jax.numpy as jnp
from jax import lax
from jax.experimental import pallas as pl
from jax.experimental.pallas import tpu as pltpu

F32 = jnp.float32
BF16 = jnp.bfloat16
EPS = 1e-6

HEAD_DIM = 128
SSM_GROUP_CH = 16
TAIL_ROWS = 128
ATTN_TILE = 256
ATTN_ROW_TILES = 2
ATTN_HEADS_PER_STEP = 4
SEQ_TILE = 128
SUBLANES = 8
HALO = 32
VMEM_LIMIT = 56 * 1024 * 1024
MOE_TILE = 1024
GATHER_ROWS = 256
DMA_UNROLL = 8


def _pick(n, prefs):
    for p in prefs:
        if n % p == 0:
            return p
    raise ValueError(f"no tile for {n} in {prefs}")


def _cparams(*sem):
    return pltpu.CompilerParams(dimension_semantics=sem, vmem_limit_bytes=VMEM_LIMIT)


def _dot(a, b):
    return jnp.dot(a, b, preferred_element_type=F32)


def _rms(x, g):
    ms = jnp.mean(x * x, axis=-1, keepdims=True)
    return x * lax.rsqrt(ms + EPS) * g


def _sigmoid(x):
    return 1.0 / (1.0 + jnp.exp(-x))


def _rms_matmul_kernel(x_ref, g_ref, w_ref, s_ref, o_ref, xn_ref):
    @pl.when(pl.program_id(1) == 0)
    def _():
        xn_ref[...] = _rms(x_ref[...], g_ref[...]).astype(BF16)

    o_ref[...] = (_dot(xn_ref[...], w_ref[...]) * s_ref[...]).astype(o_ref.dtype)


def rms_matmul(x, g, w, col_scale, out_dtype):
    R, D = x.shape
    N = w.shape[1]
    tm = _pick(R, (768, 512, 256, 128))
    tn = _pick(N, (768, 512, 384, 256, 128))
    return pl.pallas_call(
        _rms_matmul_kernel,
        grid=(R // tm, N // tn),
        in_specs=[
            pl.BlockSpec((tm, D), lambda i, j: (i, 0)),
            pl.BlockSpec((1, D), lambda i, j: (0, 0)),
            pl.BlockSpec((D, tn), lambda i, j: (0, j)),
            pl.BlockSpec((1, tn), lambda i, j: (0, j)),
        ],
        out_specs=pl.BlockSpec((tm, tn), lambda i, j: (i, j)),
        out_shape=jax.ShapeDtypeStruct((R, N), out_dtype),
        scratch_shapes=[pltpu.VMEM((tm, D), BF16)],
        compiler_params=_cparams("parallel", "arbitrary"),
        name="rms_in_proj",
    )(x, g.reshape(1, D), w, col_scale)


def _sb_block(q, kb, vb, later, car, mask):
    z = lax.dot_general(q, kb, (((1,), (1,)), ((), ())), preferred_element_type=F32)
    neg_abs = lax.bitcast_convert_type(
        lax.bitcast_convert_type(z, jnp.uint32) | jnp.uint32(0x80000000), F32)
    lb = jnp.minimum(z, 0.0) - jnp.log2(1.0 + jnp.exp2(neg_abs))
    l1m = lb - z
    if mask is not None:
        l1m = jnp.where(mask, l1m, 0.0)
    after = _dot(l1m.astype(BF16), later)
    w = jnp.exp2(lb + after + car)
    if mask is not None:
        w = jnp.where(mask, w, 0.0)
    return _dot(w.astype(BF16), vb), jnp.sum(l1m, axis=-1, keepdims=True)


def _attn_kernel(q_ref, k_ref, v_ref, tri_ref, trit_ref, o_ref, acc_ref, car_ref,
                 *, seq, n_meta, m, hp):
    i = pl.program_id(2)
    tk = ATTN_TILE
    tq = m * tk
    tb = TAIL_ROWS
    dh = HEAD_DIM
    n_real = seq // tq
    first_meta = tb - n_meta

    def step(r0, nrows, koff, nk, tri_ext, mask):
        for h in range(hp):
            cs = slice(h * dh, (h + 1) * dh)
            pv, rs = _sb_block(q_ref[0, r0:r0 + nrows, cs], k_ref[0, pl.ds(koff, nk), cs],
                               v_ref[0, pl.ds(koff, nk), cs], tri_ext,
                               car_ref[h, r0:r0 + nrows, :], mask)
            acc_ref[h, r0:r0 + nrows, :] += pv
            car_ref[h, r0:r0 + nrows, :] += rs

    acc_ref[...] = jnp.zeros_like(acc_ref)
    car_ref[...] = jnp.zeros_like(car_ref)

    @pl.when(i < n_real)
    def _():
        for d in reversed(range(m)):
            nrows = tq - d * tk
            rr = lax.broadcasted_iota(jnp.int32, (nrows, tk), 0)
            cc = lax.broadcasted_iota(jnp.int32, (nrows, tk), 1)
            step(d * tk, nrows, pl.multiple_of((i * m + d) * tk, tk), tk, tri_ref[...], cc < rr)

        def body(n, _):
            step(0, tq, pl.multiple_of((i * m - 1 - n) * tk, tk), tk, tri_ref[...], None)
            return 0

        lax.fori_loop(0, i * m, body, 0)
        ct = lax.broadcasted_iota(jnp.int32, (tq, tb), 1)
        step(0, tq, seq, tb, trit_ref[...], ct >= first_meta)
        for h in range(hp):
            o_ref[0, :, h * dh:(h + 1) * dh] = acc_ref[h]

    @pl.when(i == n_real)
    def _():
        rr = lax.broadcasted_iota(jnp.int32, (tb, tb), 0)
        cc = lax.broadcasted_iota(jnp.int32, (tb, tb), 1)
        step(0, tb, seq, tb, trit_ref[...], jnp.logical_and(cc < rr, cc >= first_meta))
        o_ref[...] = jnp.zeros_like(o_ref)
        for h in range(hp):
            o_ref[0, 0:tb, h * dh:(h + 1) * dh] = acc_ref[h, 0:tb, :]


def attention(qkv, seq, n_meta):
    B, Lp, W3 = qkv.shape
    H = W3 // (3 * HEAD_DIM)
    tk = ATTN_TILE
    m = ATTN_ROW_TILES if seq % (ATTN_ROW_TILES * tk) == 0 else 1
    hp = ATTN_HEADS_PER_STEP if H % ATTN_HEADS_PER_STEP == 0 else 1
    tq = m * tk
    wb = hp * HEAD_DIM
    ng = H // hp
    ti = jnp.arange(tk, dtype=jnp.int32)
    tri = (ti[:, None] > ti[None, :]).astype(BF16)
    trit = tri[:TAIL_ROWS, :TAIL_ROWS]
    return pl.pallas_call(
        functools.partial(_attn_kernel, seq=seq, n_meta=n_meta, m=m, hp=hp),
        grid=(B, ng, seq // tq + 1),
        in_specs=[
            pl.BlockSpec((1, tq, wb), lambda b, g, i: (b, i, g)),
            pl.BlockSpec((1, Lp, wb), lambda b, g, i: (b, 0, ng + g)),
            pl.BlockSpec((1, Lp, wb), lambda b, g, i: (b, 0, 2 * ng + g)),
            pl.BlockSpec(tri.shape, lambda b, g, i: (0, 0)),
            pl.BlockSpec(trit.shape, lambda b, g, i: (0, 0)),
        ],
        out_specs=pl.BlockSpec((1, tq, wb), lambda b, g, i: (b, i, g)),
        out_shape=jax.ShapeDtypeStruct((B, Lp, H * HEAD_DIM), F32),
        scratch_shapes=[pltpu.VMEM((hp, tq, HEAD_DIM), F32), pltpu.VMEM((hp, tq, 1), F32)],
        compiler_params=_cparams("parallel", "parallel", "arbitrary"),
        name="sb_attention",
    )(qkv, qkv, qkv, tri, trit)


def _gelu_tanh(x):
    return 0.5 * x * (1.0 + jnp.tanh(math.sqrt(2.0 / math.pi) * (x + 0.044715 * (x * x * x))))


def _s5_kernel(u_ref, bd_ref, cd_ref, pw_ref, d_ref, wglu_ref, g_ref, o_ref,
               carry_ref, st_ref, y_ref, *, nh, cw, cs):
    tc = SEQ_TILE

    @pl.when(pl.program_id(1) == 0)
    def _():
        carry_ref[...] = jnp.zeros_like(carry_ref)

    u = u_ref[0]
    ub = u.astype(BF16)
    for c in range(nh):
        st_ref[...] = _dot(ub[:, c * cw:(c + 1) * cw], bd_ref[c])

        def tile(r, carry, c=c):
            cre, cim = carry
            off = pl.multiple_of(r * SUBLANES, SUBLANES)
            xre = st_ref[pl.ds(off, SUBLANES), 0:cs]
            xim = st_ref[pl.ds(off, SUBLANES), cs:2 * cs]
            for kk, sh in enumerate((1, 2, 4)):
                pr = pw_ref[c, kk, :, 0:cs]
                pi = pw_ref[c, kk, :, cs:2 * cs]
                sre = pltpu.roll(xre, sh, axis=0)
                sim = pltpu.roll(xim, sh, axis=0)
                xre, xim = xre + pr * sre - pi * sim, xim + pr * sim + pi * sre
            pr = pw_ref[c, 3, :, 0:cs]
            pi = pw_ref[c, 3, :, cs:2 * cs]
            xre, xim = xre + pr * cre - pi * cim, xim + pr * cim + pi * cre
            st_ref[pl.ds(off, SUBLANES), 0:cs] = xre
            st_ref[pl.ds(off, SUBLANES), cs:2 * cs] = xim
            return (jnp.broadcast_to(xre[SUBLANES - 1:SUBLANES, :], (SUBLANES, cs)),
                    jnp.broadcast_to(xim[SUBLANES - 1:SUBLANES, :], (SUBLANES, cs)))

        cre, cim = lax.fori_loop(0, tc // SUBLANES, tile,
                                 (carry_ref[c, :, 0:cs], carry_ref[c, :, cs:2 * cs]))
        carry_ref[c, :, 0:cs] = cre
        carry_ref[c, :, cs:2 * cs] = cim
        y_ref[:, c * cw:(c + 1) * cw] = _dot(st_ref[...].astype(BF16), cd_ref[c])

    y = y_ref[...] + d_ref[...] * u
    g = _gelu_tanh(y)
    out = g * _sigmoid(_dot(g.astype(BF16), wglu_ref[...]))
    o_ref[0] = _rms(out, g_ref[...]).astype(BF16)


def s5_mixer(rest, bd, cd, pw, d, wglu, g):
    B, Lp, _ = rest.shape
    nh, cw, cs2 = bd.shape
    cs = cs2 // 2
    SW = nh * cw
    tc = SEQ_TILE
    nblk = Lp // tc
    seq_blk = lambda b, s: (b, (s + nblk - 1) % nblk, 0)
    full = lambda *shape: pl.BlockSpec(shape, lambda b, s: (0,) * len(shape))
    return pl.pallas_call(
        functools.partial(_s5_kernel, nh=nh, cw=cw, cs=cs),
        grid=(B, nblk),
        in_specs=[
            pl.BlockSpec((1, tc, SW), seq_blk),
            full(nh, cw, 2 * cs), full(nh, 2 * cs, cw), full(nh, 4, SUBLANES, 2 * cs),
            full(1, SW), full(SW, SW), full(1, SW),
        ],
        out_specs=pl.BlockSpec((1, tc, SW), seq_blk),
        out_shape=jax.ShapeDtypeStruct((B, Lp, SW), BF16),
        scratch_shapes=[pltpu.VMEM((nh, SUBLANES, 2 * cs), F32),
                        pltpu.VMEM((tc, 2 * cs), F32),
                        pltpu.VMEM((tc, SW), F32)],
        compiler_params=_cparams("parallel", "arbitrary"),
        name="s5_mixer",
    )(rest, bd, cd, pw, d, wglu, g)


def s5_params(lam_re, lam_im, log_dt, b_re, b_im, c_re, c_im):
    G, P = lam_re.shape
    Hc = SSM_GROUP_CH
    dt = jnp.exp(log_dt.astype(F32))[:, None]
    lr = lam_re.astype(F32)
    li = lam_im.astype(F32)
    mag = jnp.exp(lr * dt)
    ang = li * dt
    ab_re = mag * jnp.cos(ang)
    ab_im = mag * jnp.sin(ang)
    den = lr * lr + li * li
    nr = ab_re - 1.0
    ni = ab_im
    coef_re = (nr * lr + ni * li) / den
    coef_im = (ni * lr - nr * li) / den
    br = b_re.astype(F32)
    bi = b_im.astype(F32)
    bbar_re = coef_re[..., None] * br - coef_im[..., None] * bi
    bbar_im = coef_re[..., None] * bi + coef_im[..., None] * br

    SW = G * Hc
    cw = min(SW, 256)
    gc = cw // Hc
    nh = G // gc
    cs = gc * P
    eye = jnp.eye(gc, dtype=F32)

    def in_blockdiag(m):
        m = m.reshape(nh, gc, P, Hc)
        return jnp.einsum('ngph,gk->nghkp', m, eye).reshape(nh, gc * Hc, gc * P)

    def out_blockdiag(m):
        m = m.reshape(nh, gc, Hc, P)
        return jnp.einsum('nghp,gk->ngpkh', m, eye).reshape(nh, gc * P, gc * Hc)

    bd = jnp.concatenate([in_blockdiag(bbar_re), in_blockdiag(bbar_im)], axis=-1).astype(BF16)
    cd = jnp.concatenate([out_blockdiag(c_re.astype(F32)),
                          -out_blockdiag(c_im.astype(F32))], axis=1).astype(BF16)

    rows = jnp.arange(SUBLANES, dtype=F32)[:, None, None]

    def power(e):
        m = jnp.exp(e * (lr * dt)[None])
        a = e * ang[None]
        return m * jnp.cos(a), m * jnp.sin(a)

    tabs = []
    for sh in (1, 2, 4):
        pr, pi = power(jnp.full((SUBLANES, 1, 1), float(sh), F32))
        keep = rows >= sh
        tabs.append((jnp.where(keep, pr, 0.0), jnp.where(keep, pi, 0.0)))
    tabs.append(power(rows + 1.0))
    pw = jnp.stack([jnp.concatenate([pr.reshape(SUBLANES, nh, cs), pi.reshape(SUBLANES, nh, cs)],
                                    axis=-1) for pr, pi in tabs], axis=0)
    pw = pw.transpose(2, 0, 1, 3)
    return bd, cd, pw


def _conv_kernel(val_ref, gate_ref, hval_ref, hgate_ref, wdw_ref, bdw_ref, lng_ref, lnb_ref,
                 wpw_ref, ng_ref, o_ref, hbuf_ref, *, taps, nblk):
    tc = SEQ_TILE
    halo = hval_ref[0] * _sigmoid(hgate_ref[0])
    halo = jnp.where(pl.program_id(1) == nblk - 1, 0.0, halo)
    hbuf_ref[0:HALO, :] = halo
    hbuf_ref[HALO:HALO + tc, :] = val_ref[0] * _sigmoid(gate_ref[0])
    acc = jnp.broadcast_to(bdw_ref[...], o_ref.shape[1:]).astype(F32)
    base = HALO - (taps - 1)
    for b in range(SUBLANES):
        js = [j for j in range(base, base + taps) if j % SUBLANES == b]
        if not js:
            continue
        rows = tc if b == 0 else tc + SUBLANES
        g = None
        for j in js:
            term = wdw_ref[j - base:j - base + 1, :] * hbuf_ref[j - b:j - b + rows, :]
            g = term if g is None else g + term
        acc = acc + g[b:b + tc, :]
    mu = jnp.mean(acc, axis=-1, keepdims=True)
    xc = acc - mu
    var = jnp.mean(xc * xc, axis=-1, keepdims=True)
    y = xc * lax.rsqrt(var + EPS) * lng_ref[...] + lnb_ref[...]
    y = y * _sigmoid(y)
    out = _dot(y.astype(BF16), wpw_ref[...])
    o_ref[0] = _rms(out, ng_ref[...]).astype(BF16)


def conv_module(rest, col0, wdw, bdw, lng, lnb, wpw, ng):
    B, Lp, _ = rest.shape
    taps, C = wdw.shape
    tc = SEQ_TILE
    nblk = Lp // tc
    vb = col0 // C
    per = tc // HALO
    prev_tail = lambda b, p: ((p + nblk - 1) % nblk) * per + per - 1
    full = lambda *shape: pl.BlockSpec(shape, lambda b, p: (0,) * len(shape))
    return pl.pallas_call(
        functools.partial(_conv_kernel, taps=taps, nblk=nblk),
        grid=(B, nblk),
        in_specs=[
            pl.BlockSpec((1, tc, C), lambda b, p: (b, p, vb)),
            pl.BlockSpec((1, tc, C), lambda b, p: (b, p, vb + 1)),
            pl.BlockSpec((1, HALO, C), lambda b, p: (b, prev_tail(b, p), vb)),
            pl.BlockSpec((1, HALO, C), lambda b, p: (b, prev_tail(b, p), vb + 1)),
            full(taps, C), full(1, C), full(1, C), full(1, C), full(C, C), full(1, C),
        ],
        out_specs=pl.BlockSpec((1, tc, C), lambda b, p: (b, p, 0)),
        out_shape=jax.ShapeDtypeStruct((B, Lp, C), BF16),
        scratch_shapes=[pltpu.VMEM((HALO + tc, C), F32)],
        compiler_params=_cparams("parallel", "parallel"),
        name="conv_module",
    )(rest, rest, rest, rest, wdw, bdw, lng, lnb, wpw, ng)


def _outproj_kernel(attn_ref, ssm_ref, conv_ref, ag_ref, w_ref, res_ref, keep_ref, o_ref,
                    an_ref, *, aw, sw):
    @pl.when(pl.program_id(1) == 0)
    def _():
        an_ref[...] = _rms(attn_ref[...], ag_ref[...]).astype(BF16)

    acc = (_dot(an_ref[...], w_ref[0:aw, :]) + _dot(ssm_ref[...], w_ref[aw:aw + sw, :])
           + _dot(conv_ref[...], w_ref[aw + sw:, :]))
    o_ref[...] = res_ref[...] + jnp.where(keep_ref[...] > 0.0, acc, 0.0)


def out_proj(attn, ssm, conv, ag, w, res, keep):
    R, AW = attn.shape
    SW = ssm.shape[1]
    C = conv.shape[1]
    D = w.shape[1]
    tm = _pick(R, (768, 512, 256, 128))
    tn = _pick(D, (1024, 512, 256, 128))
    return pl.pallas_call(
        functools.partial(_outproj_kernel, aw=AW, sw=SW),
        grid=(R // tm, D // tn),
        in_specs=[
            pl.BlockSpec((tm, AW), lambda i, j: (i, 0)),
            pl.BlockSpec((tm, SW), lambda i, j: (i, 0)),
            pl.BlockSpec((tm, C), lambda i, j: (i, 0)),
            pl.BlockSpec((1, AW), lambda i, j: (0, 0)),
            pl.BlockSpec((AW + SW + C, tn), lambda i, j: (0, j)),
            pl.BlockSpec((tm, tn), lambda i, j: (i, j)),
            pl.BlockSpec((tm, 1), lambda i, j: (i, 0)),
        ],
        out_specs=pl.BlockSpec((tm, tn), lambda i, j: (i, j)),
        out_shape=jax.ShapeDtypeStruct((R, D), F32),
        scratch_shapes=[pltpu.VMEM((tm, AW), BF16)],
        compiler_params=_cparams("parallel", "arbitrary"),
        name="out_proj",
    )(attn, ssm, conv, ag, w, res, keep)


def _ffn_up_kernel(x_ref, g_ref, wg_ref, wu_ref, o_ref, xn_ref):
    @pl.when(pl.program_id(1) == 0)
    def _():
        xn_ref[...] = _rms(x_ref[...], g_ref[...]).astype(BF16)

    a = _dot(xn_ref[...], wg_ref[...])
    o_ref[...] = (a * _sigmoid(a) * _dot(xn_ref[...], wu_ref[...])).astype(BF16)


def ffn_up(x, g, wg, wu):
    R, D = x.shape
    F = wg.shape[1]
    tm = _pick(R, (768, 512, 256, 128))
    tf = _pick(F, (512, 256, 128))
    return pl.pallas_call(
        _ffn_up_kernel,
        grid=(R // tm, F // tf),
        in_specs=[
            pl.BlockSpec((tm, D), lambda i, j: (i, 0)),
            pl.BlockSpec((1, D), lambda i, j: (0, 0)),
            pl.BlockSpec((D, tf), lambda i, j: (0, j)),
            pl.BlockSpec((D, tf), lambda i, j: (0, j)),
        ],
        out_specs=pl.BlockSpec((tm, tf), lambda i, j: (i, j)),
        out_shape=jax.ShapeDtypeStruct((R, F), BF16),
        scratch_shapes=[pltpu.VMEM((tm, D), BF16)],
        compiler_params=_cparams("parallel", "arbitrary"),
        name="ffn_up",
    )(x, g.reshape(1, D), wg, wu)


def _ffn_down_kernel(a_ref, w_ref, res_ref, o_ref):
    o_ref[...] = res_ref[...] + _dot(a_ref[...], w_ref[...])


def ffn_down(act, w, res):
    R, F = act.shape
    D = w.shape[1]
    tm = _pick(R, (768, 512, 256, 128))
    tn = _pick(D, (512, 256, 128))
    return pl.pallas_call(
        _ffn_down_kernel,
        grid=(R // tm, D // tn),
        in_specs=[
            pl.BlockSpec((tm, F), lambda i, j: (i, 0)),
            pl.BlockSpec((F, tn), lambda i, j: (0, j)),
            pl.BlockSpec((tm, tn), lambda i, j: (i, j)),
        ],
        out_specs=pl.BlockSpec((tm, tn), lambda i, j: (i, j)),
        out_shape=jax.ShapeDtypeStruct((R, D), F32),
        compiler_params=_cparams("parallel", "parallel"),
        name="ffn_down",
    )(act, w, res)


def _router_kernel(x_ref, g_ref, wr_ref, xn_ref, idx_ref, gw_ref, *, n_exp):
    xn = _rms(x_ref[...], g_ref[...])
    xn_ref[...] = xn
    logits = jnp.dot(xn, wr_ref[...], preferred_element_type=F32,
                     precision=lax.Precision.HIGHEST)
    lane = lax.broadcasted_iota(jnp.int32, logits.shape, 1)
    lanef = lane.astype(F32)
    big = float(logits.shape[1])
    logits = jnp.where(lane < n_exp, logits, -jnp.inf)
    m1 = jnp.max(logits, axis=-1, keepdims=True)
    i1 = jnp.min(jnp.where(logits == m1, lanef, big), axis=-1, keepdims=True)
    rest = jnp.where(lanef == i1, -jnp.inf, logits)
    m2 = jnp.max(rest, axis=-1, keepdims=True)
    i2 = jnp.min(jnp.where(rest == m2, lanef, big), axis=-1, keepdims=True)
    e = jnp.exp(m2 - m1)
    w1 = 1.0 / (1.0 + e)
    w2 = e / (1.0 + e)
    idx_ref[...] = jnp.where(lane == 0, i1, jnp.where(lane == 1, i2, 0.0)).astype(jnp.int32)
    gw_ref[...] = jnp.where(lane == 0, w1, jnp.where(lane == 1, w2, 0.0))


def router(x, g, wr_pad, n_exp):
    R, D = x.shape
    tm = _pick(R, (512, 256, 128))
    NL = wr_pad.shape[1]
    return pl.pallas_call(
        functools.partial(_router_kernel, n_exp=n_exp),
        grid=(R // tm,),
        in_specs=[
            pl.BlockSpec((tm, D), lambda i: (i, 0)),
            pl.BlockSpec((1, D), lambda i: (0, 0)),
            pl.BlockSpec((D, NL), lambda i: (0, 0)),
        ],
        out_specs=[
            pl.BlockSpec((tm, D), lambda i: (i, 0)),
            pl.BlockSpec((tm, NL), lambda i: (i, 0)),
            pl.BlockSpec((tm, NL), lambda i: (i, 0)),
        ],
        out_shape=[jax.ShapeDtypeStruct((R, D), F32),
                   jax.ShapeDtypeStruct((R, NL), jnp.int32),
                   jax.ShapeDtypeStruct((R, NL), F32)],
        compiler_params=_cparams("parallel"),
        name="moe_router",
    )(x, g.reshape(1, D), wr_pad)


def _row_copy(src_ref, dst_ref, sem, s, d):
    return pltpu.make_async_copy(src_ref.at[pl.ds(s, 1), :], dst_ref.at[pl.ds(d, 1), :], sem)


def _moe_up_kernel(te_ref, nu_ref, idx0_ref, idxn_ref, xn_ref, wg_ref, wu_ref, o_ref,
                   xf_ref, xb_ref, sem):
    t = pl.program_id(0)
    tm = MOE_TILE
    n_used = nu_ref[0]

    def start_gather(idx_ref, slot):
        def body(r, _):
            pltpu.make_async_copy(xn_ref.at[pl.ds(idx_ref[0, 0, r], 1), :],
                                  xf_ref.at[slot, pl.ds(r, 1), :], sem.at[slot]).start()
            return 0

        lax.fori_loop(0, tm, body, 0, unroll=DMA_UNROLL)

    @pl.when(pl.program_id(1) == 0)
    def _():
        slot = t % 2

        @pl.when(t == 0)
        def _():
            start_gather(idx0_ref, 0)

        @pl.when(t < n_used)
        def _():
            pltpu.make_async_copy(xf_ref.at[slot], xf_ref.at[slot], sem.at[slot]).wait()
            xb_ref[...] = xf_ref[slot].astype(BF16)

        @pl.when(t + 1 < n_used)
        def _():
            start_gather(idxn_ref, 1 - slot)

    @pl.when(t < n_used)
    def _():
        a = _dot(xb_ref[...], wg_ref[0].astype(BF16))
        o_ref[...] = (a * _sigmoid(a) * _dot(xb_ref[...], wu_ref[0].astype(BF16))).astype(BF16)

    @pl.when(t >= n_used)
    def _():
        o_ref[...] = jnp.zeros_like(o_ref)


def moe_up(xn, slot_token, wg, wu, tile_expert, n_used):
    D = xn.shape[1]
    S = slot_token.shape[0]
    F = wg.shape[2]
    tm = MOE_TILE
    nt = S // tm
    tf = _pick(F, (512, 256, 128))
    nf = F // tf
    fidx = lambda t, f, te, nu: jnp.where(t < nu[0], f, nf - 1)
    idx3 = slot_token.reshape(nt, 1, tm)
    return pl.pallas_call(
        _moe_up_kernel,
        grid_spec=pltpu.PrefetchScalarGridSpec(
            num_scalar_prefetch=2,
            grid=(nt, nf),
            in_specs=[
                pl.BlockSpec((1, 1, tm), lambda t, f, te, nu: (0, 0, 0), memory_space=pltpu.SMEM),
                pl.BlockSpec((1, 1, tm), lambda t, f, te, nu: (jnp.minimum(t + 1, nt - 1), 0, 0),
                             memory_space=pltpu.SMEM),
                pl.BlockSpec(memory_space=pl.ANY),
                pl.BlockSpec((1, D, tf), lambda t, f, te, nu: (te[t], 0, fidx(t, f, te, nu))),
                pl.BlockSpec((1, D, tf), lambda t, f, te, nu: (te[t], 0, fidx(t, f, te, nu))),
            ],
            out_specs=pl.BlockSpec((tm, tf), lambda t, f, te, nu: (t, f)),
            scratch_shapes=[pltpu.VMEM((2, tm, D), F32), pltpu.VMEM((tm, D), BF16),
                            pltpu.SemaphoreType.DMA((2,))],
        ),
        out_shape=jax.ShapeDtypeStruct((S, F), BF16),
        compiler_params=_cparams("arbitrary", "arbitrary"),
        name="moe_gather_up",
    )(tile_expert, n_used, idx3, idx3, xn, wg, wu)


def _moe_down_kernel(te_ref, nu_ref, a_ref, w_ref, o_ref):
    @pl.when(pl.program_id(0) < nu_ref[0])
    def _():
        o_ref[...] = _dot(a_ref[...], w_ref[0].astype(BF16))

    @pl.when(pl.program_id(0) >= nu_ref[0])
    def _():
        o_ref[...] = jnp.zeros_like(o_ref)


def moe_down(act, wd, tile_expert, n_used):
    S, F = act.shape
    D = wd.shape[2]
    tm = MOE_TILE
    tn = _pick(D, (256, 128))
    nn = D // tn
    jidx = lambda t, j, te, nu: jnp.where(t < nu[0], j, nn - 1)
    return pl.pallas_call(
        _moe_down_kernel,
        grid_spec=pltpu.PrefetchScalarGridSpec(
            num_scalar_prefetch=2,
            grid=(S // tm, nn),
            in_specs=[
                pl.BlockSpec((tm, F), lambda t, j, te, nu: (t, 0)),
                pl.BlockSpec((1, F, tn), lambda t, j, te, nu: (te[t], 0, jidx(t, j, te, nu))),
            ],
            out_specs=pl.BlockSpec((tm, tn), lambda t, j, te, nu: (t, j)),
        ),
        out_shape=jax.ShapeDtypeStruct((S, D), F32),
        compiler_params=_cparams("arbitrary", "arbitrary"),
        name="moe_down",
    )(tile_expert, n_used, act, wd)


def _combine_kernel(p1_ref, p2_ref, y_ref, gw_ref, res_ref, o_ref, y1_ref, y2_ref, sem):
    n = GATHER_ROWS

    def start(r, _):
        _row_copy(y_ref, y1_ref, sem.at[0], p1_ref[0, 0, r], r).start()
        _row_copy(y_ref, y2_ref, sem.at[1], p2_ref[0, 0, r], r).start()
        return 0

    lax.fori_loop(0, n, start, 0, unroll=DMA_UNROLL)
    pltpu.make_async_copy(y1_ref, y1_ref, sem.at[0]).wait()
    pltpu.make_async_copy(y2_ref, y2_ref, sem.at[1]).wait()
    gw = gw_ref[...]
    o_ref[...] = res_ref[...] + gw[:, 0:1] * y1_ref[...] + gw[:, 1:2] * y2_ref[...]


def moe_combine(y, pos1, pos2, gw, res):
    R, D = res.shape
    n = GATHER_ROWS
    NL = gw.shape[1]
    smem = lambda: pl.BlockSpec((1, 1, n), lambda i: (i, 0, 0), memory_space=pltpu.SMEM)
    return pl.pallas_call(
        _combine_kernel,
        grid=(R // n,),
        in_specs=[
            smem(), smem(),
            pl.BlockSpec(memory_space=pl.ANY),
            pl.BlockSpec((n, NL), lambda i: (i, 0)),
            pl.BlockSpec((n, D), lambda i: (i, 0)),
        ],
        out_specs=pl.BlockSpec((n, D), lambda i: (i, 0)),
        out_shape=jax.ShapeDtypeStruct((R, D), F32),
        scratch_shapes=[pltpu.VMEM((n, D), F32), pltpu.VMEM((n, D), F32),
                        pltpu.SemaphoreType.DMA((2,))],
        compiler_params=_cparams("arbitrary"),
        name="moe_combine",
    )(pos1.reshape(R // n, 1, n), pos2.reshape(R // n, 1, n), y, gw, res)


def moe_dispatch(idx, n_exp):
    R = idx.shape[0]
    tm = MOE_TILE
    A = 2 * R
    n_tiles = -(-A // tm) + n_exp
    e_flat = jnp.concatenate([idx[:, 0], idx[:, 1]])
    tok = jnp.concatenate([jnp.arange(R, dtype=jnp.int32)] * 2)
    onehot = (e_flat[:, None] == jnp.arange(n_exp, dtype=jnp.int32)[None, :]).astype(jnp.int32)
    csum = jnp.cumsum(onehot, axis=0)
    rank = jnp.sum((csum - onehot) * onehot, axis=1)
    counts = csum[-1]
    ptiles = (counts + tm - 1) // tm
    tile_end = jnp.cumsum(ptiles)
    tile_off = tile_end - ptiles
    pos = (tile_off[e_flat] * tm + rank).astype(jnp.int32)
    slot_token = jnp.zeros((n_tiles * tm,), jnp.int32).at[pos].set(tok)
    n_used = tile_end[-1].astype(jnp.int32)
    t = jnp.arange(n_tiles, dtype=jnp.int32)
    tile_expert = jnp.sum((t[:, None] >= tile_end[None, :]).astype(jnp.int32), axis=1)
    last = jnp.sum((n_used - 1 >= tile_end).astype(jnp.int32))
    tile_expert = jnp.where(t < n_used, tile_expert, last).astype(jnp.int32)
    return slot_token, pos[:R], pos[R:], tile_expert, n_used.reshape(1)


def moe_layer(h_res, g, wr_pad, wg, wu, wd, n_exp, layer):
    xn, idx, gw = router(h_res, g, wr_pad, n_exp)
    slot_token, pos1, pos2, tile_expert, n_used = moe_dispatch(idx[:, :2], n_exp)
    tile_expert = tile_expert + layer * n_exp
    act = moe_up(xn, slot_token, wg, wu, tile_expert, n_used)
    y = moe_down(act, wd, tile_expert, n_used)
    return moe_combine(y, pos1, pos2, gw, h_res)


def _final_kernel(x_ref, g_ref, o_ref):
    o_ref[0] = _rms(x_ref[0], g_ref[...])


def final_norm(h3, g, seq):
    B, Lp, D = h3.shape
    tm = _pick(seq, (512, 256, 128))
    return pl.pallas_call(
        _final_kernel,
        grid=(B, seq // tm),
        in_specs=[pl.BlockSpec((1, tm, D), lambda b, i: (b, i, 0)),
                  pl.BlockSpec((1, D), lambda b, i: (0, 0))],
        out_specs=pl.BlockSpec((1, tm, D), lambda b, i: (b, i, 0)),
        out_shape=jax.ShapeDtypeStruct((B, seq, D), F32),
        compiler_params=_cparams("parallel", "parallel"),
        name="final_norm",
    )(h3, g.reshape(1, D))


def kernel(x, meta_tokens, norm_mix_g, w_in, w_out, attn_norm_g, ssm_lambda_re, ssm_lambda_im, ssm_log_dt, ssm_b_re, ssm_b_im, ssm_c_re, ssm_c_im, ssm_d, ssm_w_glu, ssm_norm_g, conv_w_dw, conv_b_dw, conv_ln_g, conv_ln_b, conv_w_pw, conv_norm_g, norm_ffn_g, dense_w_gate, dense_w_up, dense_w_down, moe_w_router, moe_w_gate, moe_w_up, moe_w_down, final_norm_g):
    B, seq, D = x.shape
    n_meta = meta_tokens.shape[0]
    depth = w_in.shape[0]
    AW = attn_norm_g.shape[1]
    SW = ssm_d.shape[1]
    C = conv_b_dw.shape[1]
    n_exp = moe_w_router.shape[2]
    Lp = seq + TAIL_ROWS
    R = B * Lp
    assert seq % ATTN_TILE == 0 and n_meta <= TAIL_ROWS and AW % HEAD_DIM == 0
    assert conv_w_dw.shape[1] - 1 <= HALO and SW == C

    meta = jnp.broadcast_to(meta_tokens.astype(F32)[None], (B, n_meta, D))
    pad = jnp.zeros((B, TAIL_ROWS - n_meta, D), F32)
    h_res = jnp.concatenate([x.astype(F32), pad, meta], axis=1).reshape(R, D)
    local = jnp.arange(Lp, dtype=jnp.int32)
    keep = jnp.logical_or(local < seq, local >= Lp - n_meta).astype(F32)
    keep = jnp.broadcast_to(keep[None, :], (B, Lp)).reshape(R, 1)

    q_scale = HEAD_DIM ** -0.5 * math.log2(math.e)
    qkv_scale = jnp.concatenate([jnp.full((1, AW), q_scale, F32), jnp.ones((1, 2 * AW), F32)], axis=1)
    rest_scale = jnp.ones((1, SW + 2 * C), F32)
    NL = 128
    bf = lambda a: a.astype(BF16)
    d_ff = moe_w_gate.shape[3]
    moe_wg = moe_w_gate.astype(F32).reshape(-1, D, d_ff)
    moe_wu = moe_w_up.astype(F32).reshape(-1, D, d_ff)
    moe_wd = moe_w_down.astype(F32).reshape(-1, d_ff, D)

    for i in range(depth):
        w_qkv = bf(w_in[i, :, :3 * AW])
        w_rest = bf(w_in[i, :, 3 * AW:])
        qkv = rms_matmul(h_res, norm_mix_g[i], w_qkv, qkv_scale, BF16)
        rest = rms_matmul(h_res, norm_mix_g[i], w_rest, rest_scale, F32)
        attn = attention(qkv.reshape(B, Lp, 3 * AW), seq, n_meta)
        bd, cd, pw = s5_params(ssm_lambda_re[i], ssm_lambda_im[i], ssm_log_dt[i], ssm_b_re[i],
                               ssm_b_im[i], ssm_c_re[i], ssm_c_im[i])
        rest3 = rest.reshape(B, Lp, SW + 2 * C)
        ssm = s5_mixer(rest3, bd, cd, pw, ssm_d[i].reshape(1, SW).astype(F32), bf(ssm_w_glu[i]),
                       ssm_norm_g[i].reshape(1, SW).astype(F32))
        conv = conv_module(rest3, SW, conv_w_dw[i].astype(F32), conv_b_dw[i].reshape(1, C).astype(F32),
                           conv_ln_g[i].reshape(1, C).astype(F32), conv_ln_b[i].reshape(1, C).astype(F32),
                           bf(conv_w_pw[i]), conv_norm_g[i].reshape(1, C).astype(F32))
        h_res = out_proj(attn.reshape(R, AW), ssm.reshape(R, SW), conv.reshape(R, C),
                         attn_norm_g[i].reshape(1, AW).astype(F32), bf(w_out[i]), h_res, keep)
        j = i // 2
        if i % 2 == 0:
            act = ffn_up(h_res, norm_ffn_g[i], bf(dense_w_gate[j]), bf(dense_w_up[j]))
            h_res = ffn_down(act, bf(dense_w_down[j]), h_res)
        else:
            wr_pad = jnp.zeros((D, NL), F32).at[:, :n_exp].set(moe_w_router[j].astype(F32))
            h_res = moe_layer(h_res, norm_ffn_g[i], wr_pad, moe_wg, moe_wu, moe_wd, n_exp, j)

    return final_norm(h_res.reshape(B, Lp, D), final_norm_g.astype(F32), seq).astype(x.dtype)
```

```python
import functools
import math

import jax
import jax.numpy as jnp
from jax import lax
from jax.experimental import pallas as pl
from jax.experimental.pallas import tpu as pltpu

F32 = jnp.float32
BF16 = jnp.bfloat16
EPS = 1e-6

HEAD_DIM = 128
SSM_GROUP_CH = 16
TAIL_ROWS = 128
ATTN_TILE = 256
ATTN_ROW_TILES = 2
ATTN_HEADS_PER_STEP = 4
SEQ_TILE = 128
SUBLANES = 8
HALO = 32
VMEM_LIMIT = 56 * 1024 * 1024
MOE_TILE = 1024
GATHER_ROWS = 256
DMA_UNROLL = 8


def _pick(n, prefs):
    for p in prefs:
        if n % p == 0:
            return p
    raise ValueError(f"no tile for {n} in {prefs}")


def _cparams(*sem):
    return pltpu.CompilerParams(dimension_semantics=sem, vmem_limit_bytes=VMEM_LIMIT)


def _dot(a, b):
    return jnp.dot(a, b, preferred_element_type=F32)


def _rms(x, g):
    ms = jnp.mean(x * x, axis=-1, keepdims=True)
    return x * lax.rsqrt(ms + EPS) * g


def _sigmoid(x):
    return 1.0 / (1.0 + jnp.exp(-x))


def _rms_matmul_kernel(x_ref, g_ref, w_ref, s_ref, oa_ref, ob_ref, xn_ref, *, na):
    j = pl.program_id(1)

    @pl.when(j == 0)
    def _():
        xn_ref[...] = _rms(x_ref[...], g_ref[...]).astype(BF16)

    acc = _dot(xn_ref[...], w_ref[...]) * s_ref[...]

    @pl.when(j < na)
    def _():
        oa_ref[...] = acc.astype(oa_ref.dtype)

    @pl.when(j >= na)
    def _():
        ob_ref[...] = acc.astype(ob_ref.dtype)


def rms_matmul(x, g, w, col_scale, n_a):
    R, D = x.shape
    N = w.shape[1]
    tm = _pick(R, (768, 512, 256, 128))
    tn = _pick(math.gcd(n_a, N - n_a), (768, 512, 384, 256, 128))
    na = n_a // tn
    return pl.pallas_call(
        functools.partial(_rms_matmul_kernel, na=na),
        grid=(R // tm, N // tn),
        in_specs=[
            pl.BlockSpec((tm, D), lambda i, j: (i, 0)),
            pl.BlockSpec((1, D), lambda i, j: (0, 0)),
            pl.BlockSpec((D, tn), lambda i, j: (0, j)),
            pl.BlockSpec((1, tn), lambda i, j: (0, j)),
        ],
        out_specs=[pl.BlockSpec((tm, tn), lambda i, j: (i, jnp.minimum(j, na - 1))),
                   pl.BlockSpec((tm, tn), lambda i, j: (i, jnp.maximum(j - na, 0)))],
        out_shape=[jax.ShapeDtypeStruct((R, n_a), BF16),
                   jax.ShapeDtypeStruct((R, N - n_a), F32)],
        scratch_shapes=[pltpu.VMEM((tm, D), BF16)],
        compiler_params=_cparams("parallel", "arbitrary"),
        name="rms_in_proj",
    )(x, g.reshape(1, D), w, col_scale)


def _sb_logits(q, kb, mask):
    z = lax.dot_general(q, kb, (((1,), (1,)), ((), ())), preferred_element_type=F32)
    neg_abs = lax.bitcast_convert_type(
        lax.bitcast_convert_type(z, jnp.uint32) | jnp.uint32(0x80000000), F32)
    lb = jnp.minimum(z, 0.0) - jnp.log2(1.0 + jnp.exp2(neg_abs))
    l1m = lb - z
    if mask is not None:
        l1m = jnp.where(mask, l1m, 0.0)
    return lb, l1m.astype(BF16), jnp.sum(l1m, axis=-1, keepdims=True)


def _sb_weights(lb, l1m, vb, later, car, mask):
    after = _dot(l1m, later)
    w = jnp.exp2(lb + after + car)
    if mask is not None:
        w = jnp.where(mask, w, 0.0)
    return _dot(w.astype(BF16), vb)


def _sb_block(q, kb, vb, later, car, mask):
    lb, l1m, rs = _sb_logits(q, kb, mask)
    return _sb_weights(lb, l1m, vb, later, car, mask), rs


def _attn_kernel(q_ref, k_ref, v_ref, tri_ref, trit_ref, o_ref, acc_ref, car_ref,
                 *, seq, n_meta, m, hp):
    i = pl.program_id(2)
    tk = ATTN_TILE
    tq = m * tk
    tb = TAIL_ROWS
    dh = HEAD_DIM
    n_real = seq // tq
    first_meta = tb - n_meta

    def step(r0, nrows, koff, nk, tri_ext, mask):
        for h in range(hp):
            cs = slice(h * dh, (h + 1) * dh)
            pv, rs = _sb_block(q_ref[0, r0:r0 + nrows, cs], k_ref[0, pl.ds(koff, nk), cs],
                               v_ref[0, pl.ds(koff, nk), cs], tri_ext,
                               car_ref[h, r0:r0 + nrows, :], mask)
            acc_ref[h, r0:r0 + nrows, :] += pv
            car_ref[h, r0:r0 + nrows, :] += rs

    acc_ref[...] = jnp.zeros_like(acc_ref)
    car_ref[...] = jnp.zeros_like(car_ref)

    @pl.when(i < n_real)
    def _():
        for d in reversed(range(m)):
            nrows = tq - d * tk
            rr = lax.broadcasted_iota(jnp.int32, (nrows, tk), 0)
            cc = lax.broadcasted_iota(jnp.int32, (nrows, tk), 1)
            step(d * tk, nrows, pl.multiple_of((i * m + d) * tk, tk), tk, tri_ref[...], cc < rr)

        def body(n, _):
            step(0, tq, pl.multiple_of((i * m - 1 - n) * tk, tk), tk, tri_ref[...], None)
            return 0

        lax.fori_loop(0, i * m, body, 0)
        ct = lax.broadcasted_iota(jnp.int32, (tq, tb), 1)
        step(0, tq, seq, tb, trit_ref[...], ct >= first_meta)
        for h in range(hp):
            o_ref[0, :, h * dh:(h + 1) * dh] = acc_ref[h]

    @pl.when(i == n_real)
    def _():
        rr = lax.broadcasted_iota(jnp.int32, (tb, tb), 0)
        cc = lax.broadcasted_iota(jnp.int32, (tb, tb), 1)
        step(0, tb, seq, tb, trit_ref[...], jnp.logical_and(cc < rr, cc >= first_meta))
        o_ref[...] = jnp.zeros_like(o_ref)
        for h in range(hp):
            o_ref[0, 0:tb, h * dh:(h + 1) * dh] = acc_ref[h, 0:tb, :]


def attention(qkv, seq, n_meta):
    B, Lp, W3 = qkv.shape
    H = W3 // (3 * HEAD_DIM)
    tk = ATTN_TILE
    m = ATTN_ROW_TILES if seq % (ATTN_ROW_TILES * tk) == 0 else 1
    hp = ATTN_HEADS_PER_STEP if H % ATTN_HEADS_PER_STEP == 0 else 1
    tq = m * tk
    wb = hp * HEAD_DIM
    ng = H // hp
    ti = jnp.arange(tk, dtype=jnp.int32)
    tri = (ti[:, None] > ti[None, :]).astype(BF16)
    trit = tri[:TAIL_ROWS, :TAIL_ROWS]
    return pl.pallas_call(
        functools.partial(_attn_kernel, seq=seq, n_meta=n_meta, m=m, hp=hp),
        grid=(B, ng, seq // tq + 1),
        in_specs=[
            pl.BlockSpec((1, tq, wb), lambda b, g, i: (b, i, g)),
            pl.BlockSpec((1, Lp, wb), lambda b, g, i: (b, 0, ng + g)),
            pl.BlockSpec((1, Lp, wb), lambda b, g, i: (b, 0, 2 * ng + g)),
            pl.BlockSpec(tri.shape, lambda b, g, i: (0, 0)),
            pl.BlockSpec(trit.shape, lambda b, g, i: (0, 0)),
        ],
        out_specs=pl.BlockSpec((1, tq, wb), lambda b, g, i: (b, i, g)),
        out_shape=jax.ShapeDtypeStruct((B, Lp, H * HEAD_DIM), F32),
        scratch_shapes=[pltpu.VMEM((hp, tq, HEAD_DIM), F32), pltpu.VMEM((hp, tq, 1), F32)],
        compiler_params=_cparams("parallel", "parallel", "arbitrary"),
        name="sb_attention",
    )(qkv, qkv, qkv, tri, trit)


def _gelu_tanh(x):
    return 0.5 * x * (1.0 + jnp.tanh(math.sqrt(2.0 / math.pi) * (x + 0.044715 * (x * x * x))))


def _s5_kernel(u_ref, bd_ref, cd_ref, pw_ref, d_ref, wglu_ref, g_ref, o_ref,
               carry_ref, st_ref, y_ref, *, nh, cw, cs):
    tc = SEQ_TILE

    @pl.when(pl.program_id(1) == 0)
    def _():
        carry_ref[...] = jnp.zeros_like(carry_ref)

    u = u_ref[0]
    ub = u.astype(BF16)
    for c in range(nh):
        st_ref[...] = _dot(ub[:, c * cw:(c + 1) * cw], bd_ref[c])

        def tile(r, carry, c=c):
            cre, cim = carry
            off = pl.multiple_of(r * SUBLANES, SUBLANES)
            xre = st_ref[pl.ds(off, SUBLANES), 0:cs]
            xim = st_ref[pl.ds(off, SUBLANES), cs:2 * cs]
            for kk, sh in enumerate((1, 2, 4)):
                pr = pw_ref[c, kk, :, 0:cs]
                pi = pw_ref[c, kk, :, cs:2 * cs]
                sre = pltpu.roll(xre, sh, axis=0)
                sim = pltpu.roll(xim, sh, axis=0)
                xre, xim = xre + pr * sre - pi * sim, xim + pr * sim + pi * sre
            pr = pw_ref[c, 3, :, 0:cs]
            pi = pw_ref[c, 3, :, cs:2 * cs]
            xre, xim = xre + pr * cre - pi * cim, xim + pr * cim + pi * cre
            st_ref[pl.ds(off, SUBLANES), 0:cs] = xre
            st_ref[pl.ds(off, SUBLANES), cs:2 * cs] = xim
            return (jnp.broadcast_to(xre[SUBLANES - 1:SUBLANES, :], (SUBLANES, cs)),
                    jnp.broadcast_to(xim[SUBLANES - 1:SUBLANES, :], (SUBLANES, cs)))

        cre, cim = lax.fori_loop(0, tc // SUBLANES, tile,
                                 (carry_ref[c, :, 0:cs], carry_ref[c, :, cs:2 * cs]))
        carry_ref[c, :, 0:cs] = cre
        carry_ref[c, :, cs:2 * cs] = cim
        y_ref[:, c * cw:(c + 1) * cw] = _dot(st_ref[...].astype(BF16), cd_ref[c])

    y = y_ref[...] + d_ref[...] * u
    g = _gelu_tanh(y)
    out = g * _sigmoid(_dot(g.astype(BF16), wglu_ref[...]))
    o_ref[0] = _rms(out, g_ref[...]).astype(BF16)


def s5_mixer(rest, bd, cd, pw, d, wglu, g):
    B, Lp, _ = rest.shape
    nh, cw, cs2 = bd.shape
    cs = cs2 // 2
    SW = nh * cw
    tc = SEQ_TILE
    nblk = Lp // tc
    seq_blk = lambda b, s: (b, (s + nblk - 1) % nblk, 0)
    full = lambda *shape: pl.BlockSpec(shape, lambda b, s: (0,) * len(shape))
    return pl.pallas_call(
        functools.partial(_s5_kernel, nh=nh, cw=cw, cs=cs),
        grid=(B, nblk),
        in_specs=[
            pl.BlockSpec((1, tc, SW), seq_blk),
            full(nh, cw, 2 * cs), full(nh, 2 * cs, cw), full(nh, 4, SUBLANES, 2 * cs),
            full(1, SW), full(SW, SW), full(1, SW),
        ],
        out_specs=pl.BlockSpec((1, tc, SW), seq_blk),
        out_shape=jax.ShapeDtypeStruct((B, Lp, SW), BF16),
        scratch_shapes=[pltpu.VMEM((nh, SUBLANES, 2 * cs), F32),
                        pltpu.VMEM((tc, 2 * cs), F32),
                        pltpu.VMEM((tc, SW), F32)],
        compiler_params=_cparams("parallel", "arbitrary"),
        name="s5_mixer",
    )(rest, bd, cd, pw, d, wglu, g)


def s5_params(lam_re, lam_im, log_dt, b_re, b_im, c_re, c_im):
    G, P = lam_re.shape
    Hc = SSM_GROUP_CH
    dt = jnp.exp(log_dt.astype(F32))[:, None]
    lr = lam_re.astype(F32)
    li = lam_im.astype(F32)
    mag = jnp.exp(lr * dt)
    ang = li * dt
    ab_re = mag * jnp.cos(ang)
    ab_im = mag * jnp.sin(ang)
    den = lr * lr + li * li
    nr = ab_re - 1.0
    ni = ab_im
    coef_re = (nr * lr + ni * li) / den
    coef_im = (ni * lr - nr * li) / den
    br = b_re.astype(F32)
    bi = b_im.astype(F32)
    bbar_re = coef_re[..., None] * br - coef_im[..., None] * bi
    bbar_im = coef_re[..., None] * bi + coef_im[..., None] * br

    SW = G * Hc
    cw = min(SW, 256)
    gc = cw // Hc
    nh = G // gc
    cs = gc * P
    eye = jnp.eye(gc, dtype=F32)

    def in_blockdiag(m):
        m = m.reshape(nh, gc, P, Hc)
        return jnp.einsum('ngph,gk->nghkp', m, eye).reshape(nh, gc * Hc, gc * P)

    def out_blockdiag(m):
        m = m.reshape(nh, gc, Hc, P)
        return jnp.einsum('nghp,gk->ngpkh', m, eye).reshape(nh, gc * P, gc * Hc)

    bd = jnp.concatenate([in_blockdiag(bbar_re), in_blockdiag(bbar_im)], axis=-1).astype(BF16)
    cd = jnp.concatenate([out_blockdiag(c_re.astype(F32)),
                          -out_blockdiag(c_im.astype(F32))], axis=1).astype(BF16)

    rows = jnp.arange(SUBLANES, dtype=F32)[:, None, None]

    def power(e):
        m = jnp.exp(e * (lr * dt)[None])
        a = e * ang[None]
        return m * jnp.cos(a), m * jnp.sin(a)

    tabs = []
    for sh in (1, 2, 4):
        pr, pi = power(jnp.full((SUBLANES, 1, 1), float(sh), F32))
        keep = rows >= sh
        tabs.append((jnp.where(keep, pr, 0.0), jnp.where(keep, pi, 0.0)))
    tabs.append(power(rows + 1.0))
    pw = jnp.stack([jnp.concatenate([pr.reshape(SUBLANES, nh, cs), pi.reshape(SUBLANES, nh, cs)],
                                    axis=-1) for pr, pi in tabs], axis=0)
    pw = pw.transpose(2, 0, 1, 3)
    return bd, cd, pw


def _conv_kernel(val_ref, gate_ref, hval_ref, hgate_ref, wdw_ref, bdw_ref, lng_ref, lnb_ref,
                 wpw_ref, ng_ref, o_ref, hbuf_ref, *, taps, nblk):
    tc = SEQ_TILE
    halo = hval_ref[0] * _sigmoid(hgate_ref[0])
    halo = jnp.where(pl.program_id(1) == nblk - 1, 0.0, halo)
    hbuf_ref[0:HALO, :] = halo
    hbuf_ref[HALO:HALO + tc, :] = val_ref[0] * _sigmoid(gate_ref[0])
    acc = jnp.broadcast_to(bdw_ref[...], o_ref.shape[1:]).astype(F32)
    base = HALO - (taps - 1)
    for b in range(SUBLANES):
        js = [j for j in range(base, base + taps) if j % SUBLANES == b]
        if not js:
            continue
        rows = tc if b == 0 else tc + SUBLANES
        g = None
        for j in js:
            term = wdw_ref[j - base:j - base + 1, :] * hbuf_ref[j - b:j - b + rows, :]
            g = term if g is None else g + term
        acc = acc + g[b:b + tc, :]
    mu = jnp.mean(acc, axis=-1, keepdims=True)
    xc = acc - mu
    var = jnp.mean(xc * xc, axis=-1, keepdims=True)
    y = xc * lax.rsqrt(var + EPS) * lng_ref[...] + lnb_ref[...]
    y = y * _sigmoid(y)
    out = _dot(y.astype(BF16), wpw_ref[...])
    o_ref[0] = _rms(out, ng_ref[...]).astype(BF16)


def conv_module(rest, col0, wdw, bdw, lng, lnb, wpw, ng):
    B, Lp, _ = rest.shape
    taps, C = wdw.shape
    tc = SEQ_TILE
    nblk = Lp // tc
    vb = col0 // C
    per = tc // HALO
    prev_tail = lambda b, p: ((p + nblk - 1) % nblk) * per + per - 1
    full = lambda *shape: pl.BlockSpec(shape, lambda b, p: (0,) * len(shape))
    return pl.pallas_call(
        functools.partial(_conv_kernel, taps=taps, nblk=nblk),
        grid=(B, nblk),
        in_specs=[
            pl.BlockSpec((1, tc, C), lambda b, p: (b, p, vb)),
            pl.BlockSpec((1, tc, C), lambda b, p: (b, p, vb + 1)),
            pl.BlockSpec((1, HALO, C), lambda b, p: (b, prev_tail(b, p), vb)),
            pl.BlockSpec((1, HALO, C), lambda b, p: (b, prev_tail(b, p), vb + 1)),
            full(taps, C), full(1, C), full(1, C), full(1, C), full(C, C), full(1, C),
        ],
        out_specs=pl.BlockSpec((1, tc, C), lambda b, p: (b, p, 0)),
        out_shape=jax.ShapeDtypeStruct((B, Lp, C), BF16),
        scratch_shapes=[pltpu.VMEM((HALO + tc, C), F32)],
        compiler_params=_cparams("parallel", "parallel"),
        name="conv_module",
    )(rest, rest, rest, rest, wdw, bdw, lng, lnb, wpw, ng)


def _outproj_kernel(attn_ref, ssm_ref, conv_ref, ag_ref, w_ref, res_ref, keep_ref, o_ref,
                    an_ref, *, aw, sw):
    @pl.when(pl.program_id(1) == 0)
    def _():
        an_ref[...] = _rms(attn_ref[...], ag_ref[...]).astype(BF16)

    acc = (_dot(an_ref[...], w_ref[0:aw, :]) + _dot(ssm_ref[...], w_ref[aw:aw + sw, :])
           + _dot(conv_ref[...], w_ref[aw + sw:, :]))
    o_ref[...] = res_ref[...] + jnp.where(keep_ref[...] > 0.0, acc, 0.0)


def out_proj(attn, ssm, conv, ag, w, res, keep):
    R, AW = attn.shape
    SW = ssm.shape[1]
    C = conv.shape[1]
    D = w.shape[1]
    tm = _pick(R, (768, 512, 256, 128))
    tn = _pick(D, (1024, 512, 256, 128))
    return pl.pallas_call(
        functools.partial(_outproj_kernel, aw=AW, sw=SW),
        grid=(R // tm, D // tn),
        in_specs=[
            pl.BlockSpec((tm, AW), lambda i, j: (i, 0)),
            pl.BlockSpec((tm, SW), lambda i, j: (i, 0)),
            pl.BlockSpec((tm, C), lambda i, j: (i, 0)),
            pl.BlockSpec((1, AW), lambda i, j: (0, 0)),
            pl.BlockSpec((AW + SW + C, tn), lambda i, j: (0, j)),
            pl.BlockSpec((tm, tn), lambda i, j: (i, j)),
            pl.BlockSpec((tm, 1), lambda i, j: (i, 0)),
        ],
        out_specs=pl.BlockSpec((tm, tn), lambda i, j: (i, j)),
        out_shape=jax.ShapeDtypeStruct((R, D), F32),
        scratch_shapes=[pltpu.VMEM((tm, AW), BF16)],
        compiler_params=_cparams("parallel", "arbitrary"),
        name="out_proj",
    )(attn, ssm, conv, ag, w, res, keep)


def _ffn_up_kernel(x_ref, g_ref, wg_ref, wu_ref, o_ref, xn_ref):
    @pl.when(pl.program_id(1) == 0)
    def _():
        xn_ref[...] = _rms(x_ref[...], g_ref[...]).astype(BF16)

    a = _dot(xn_ref[...], wg_ref[...])
    o_ref[...] = (a * _sigmoid(a) * _dot(xn_ref[...], wu_ref[...])).astype(BF16)


def ffn_up(x, g, wg, wu):
    R, D = x.shape
    F = wg.shape[1]
    tm = _pick(R, (768, 512, 256, 128))
    tf = _pick(F, (512, 256, 128))
    return pl.pallas_call(
        _ffn_up_kernel,
        grid=(R // tm, F // tf),
        in_specs=[
            pl.BlockSpec((tm, D), lambda i, j: (i, 0)),
            pl.BlockSpec((1, D), lambda i, j: (0, 0)),
            pl.BlockSpec((D, tf), lambda i, j: (0, j)),
            pl.BlockSpec((D, tf), lambda i, j: (0, j)),
        ],
        out_specs=pl.BlockSpec((tm, tf), lambda i, j: (i, j)),
        out_shape=jax.ShapeDtypeStruct((R, F), BF16),
        scratch_shapes=[pltpu.VMEM((tm, D), BF16)],
        compiler_params=_cparams("parallel", "arbitrary"),
        name="ffn_up",
    )(x, g.reshape(1, D), wg, wu)


def _ffn_down_kernel(a_ref, w_ref, res_ref, o_ref):
    o_ref[...] = res_ref[...] + _dot(a_ref[...], w_ref[...])


def ffn_down(act, w, res):
    R, F = act.shape
    D = w.shape[1]
    tm = _pick(R, (768, 512, 256, 128))
    tn = _pick(D, (512, 256, 128))
    return pl.pallas_call(
        _ffn_down_kernel,
        grid=(R // tm, D // tn),
        in_specs=[
            pl.BlockSpec((tm, F), lambda i, j: (i, 0)),
            pl.BlockSpec((F, tn), lambda i, j: (0, j)),
            pl.BlockSpec((tm, tn), lambda i, j: (i, j)),
        ],
        out_specs=pl.BlockSpec((tm, tn), lambda i, j: (i, j)),
        out_shape=jax.ShapeDtypeStruct((R, D), F32),
        compiler_params=_cparams("parallel", "parallel"),
        name="ffn_down",
    )(act, w, res)


def _router_kernel(x_ref, g_ref, wr_ref, xn_ref, idx_ref, gw_ref, *, n_exp):
    xn = _rms(x_ref[...], g_ref[...])
    xn_ref[...] = xn
    logits = jnp.dot(xn, wr_ref[...], preferred_element_type=F32,
                     precision=lax.Precision.HIGHEST)
    lane = lax.broadcasted_iota(jnp.int32, logits.shape, 1)
    lanef = lane.astype(F32)
    big = float(logits.shape[1])
    logits = jnp.where(lane < n_exp, logits, -jnp.inf)
    m1 = jnp.max(logits, axis=-1, keepdims=True)
    i1 = jnp.min(jnp.where(logits == m1, lanef, big), axis=-1, keepdims=True)
    rest = jnp.where(lanef == i1, -jnp.inf, logits)
    m2 = jnp.max(rest, axis=-1, keepdims=True)
    i2 = jnp.min(jnp.where(rest == m2, lanef, big), axis=-1, keepdims=True)
    e = jnp.exp(m2 - m1)
    w1 = 1.0 / (1.0 + e)
    w2 = e / (1.0 + e)
    idx_ref[...] = jnp.where(lane == 0, i1, jnp.where(lane == 1, i2, 0.0)).astype(jnp.int32)
    gw_ref[...] = jnp.where(lane == 0, w1, jnp.where(lane == 1, w2, 0.0))


def router(x, g, wr_pad, n_exp):
    R, D = x.shape
    tm = _pick(R, (512, 256, 128))
    NL = wr_pad.shape[1]
    return pl.pallas_call(
        functools.partial(_router_kernel, n_exp=n_exp),
        grid=(R // tm,),
        in_specs=[
            pl.BlockSpec((tm, D), lambda i: (i, 0)),
            pl.BlockSpec((1, D), lambda i: (0, 0)),
            pl.BlockSpec((D, NL), lambda i: (0, 0)),
        ],
        out_specs=[
            pl.BlockSpec((tm, D), lambda i: (i, 0)),
            pl.BlockSpec((tm, NL), lambda i: (i, 0)),
            pl.BlockSpec((tm, NL), lambda i: (i, 0)),
        ],
        out_shape=[jax.ShapeDtypeStruct((R, D), F32),
                   jax.ShapeDtypeStruct((R, NL), jnp.int32),
                   jax.ShapeDtypeStruct((R, NL), F32)],
        compiler_params=_cparams("parallel"),
        name="moe_router",
    )(x, g.reshape(1, D), wr_pad)


def _row_copy(src_ref, dst_ref, sem, s, d):
    return pltpu.make_async_copy(src_ref.at[pl.ds(s, 1), :], dst_ref.at[pl.ds(d, 1), :], sem)


def _moe_up_kernel(te_ref, nu_ref, tr_ref, idx0_ref, idxn_ref, xn_ref, wg_ref, wu_ref, o_ref,
                   xf_ref, xb_ref, sem):
    t = pl.program_id(0)
    tm = MOE_TILE
    half = tm // 2
    n_used = nu_ref[0]
    rows = tr_ref[t]

    def start_gather(idx_ref, slot):
        def body(r, _):
            pltpu.make_async_copy(xn_ref.at[pl.ds(idx_ref[0, 0, r], 1), :],
                                  xf_ref.at[slot, pl.ds(r, 1), :], sem.at[slot]).start()
            return 0

        lax.fori_loop(0, tm, body, 0, unroll=DMA_UNROLL)

    @pl.when(pl.program_id(1) == 0)
    def _():
        slot = t % 2

        @pl.when(t == 0)
        def _():
            start_gather(idx0_ref, 0)

        @pl.when(t < n_used)
        def _():
            pltpu.make_async_copy(xf_ref.at[slot], xf_ref.at[slot], sem.at[slot]).wait()
            xb_ref[...] = xf_ref[slot].astype(BF16)

        @pl.when(t + 1 < n_used)
        def _():
            start_gather(idxn_ref, 1 - slot)

    def swiglu_rows(n):
        a = _dot(xb_ref[0:n, :], wg_ref[0].astype(BF16))
        return (a * _sigmoid(a) * _dot(xb_ref[0:n, :], wu_ref[0].astype(BF16))).astype(BF16)

    @pl.when(rows > half)
    def _():
        o_ref[...] = swiglu_rows(tm)

    @pl.when(jnp.logical_and(rows > 0, rows <= half))
    def _():
        o_ref[0:half, :] = swiglu_rows(half)
        o_ref[half:tm, :] = jnp.zeros((tm - half, o_ref.shape[1]), BF16)

    @pl.when(rows == 0)
    def _():
        o_ref[...] = jnp.zeros_like(o_ref)


def moe_up(xn, slot_token, wg, wu, tile_expert, n_used, tile_rows):
    D = xn.shape[1]
    S = slot_token.shape[0]
    F = wg.shape[2]
    tm = MOE_TILE
    nt = S // tm
    tf = _pick(F, (512, 256, 128))
    nf = F // tf
    fidx = lambda t, f, te, nu, tr: jnp.where(t < nu[0], f, nf - 1)
    idx3 = slot_token.reshape(nt, 1, tm)
    return pl.pallas_call(
        _moe_up_kernel,
        grid_spec=pltpu.PrefetchScalarGridSpec(
            num_scalar_prefetch=3,
            grid=(nt, nf),
            in_specs=[
                pl.BlockSpec((1, 1, tm), lambda t, f, te, nu, tr: (0, 0, 0), memory_space=pltpu.SMEM),
                pl.BlockSpec((1, 1, tm), lambda t, f, te, nu, tr: (jnp.minimum(t + 1, nt - 1), 0, 0),
                             memory_space=pltpu.SMEM),
                pl.BlockSpec(memory_space=pl.ANY),
                pl.BlockSpec((1, D, tf), lambda t, f, te, nu, tr: (te[t], 0, fidx(t, f, te, nu, tr))),
                pl.BlockSpec((1, D, tf), lambda t, f, te, nu, tr: (te[t], 0, fidx(t, f, te, nu, tr))),
            ],
            out_specs=pl.BlockSpec((tm, tf), lambda t, f, te, nu, tr: (t, f)),
            scratch_shapes=[pltpu.VMEM((2, tm, D), F32), pltpu.VMEM((tm, D), BF16),
                            pltpu.SemaphoreType.DMA((2,))],
        ),
        out_shape=jax.ShapeDtypeStruct((S, F), BF16),
        compiler_params=_cparams("arbitrary", "arbitrary"),
        name="moe_gather_up",
    )(tile_expert, n_used, tile_rows, idx3, idx3, xn, wg, wu)


def _moe_down_kernel(te_ref, nu_ref, tr_ref, a_ref, w_ref, o_ref):
    tm = MOE_TILE
    half = tm // 2
    rows = tr_ref[pl.program_id(0)]

    @pl.when(rows > half)
    def _():
        o_ref[...] = _dot(a_ref[...], w_ref[0].astype(BF16))

    @pl.when(jnp.logical_and(rows > 0, rows <= half))
    def _():
        o_ref[0:half, :] = _dot(a_ref[0:half, :], w_ref[0].astype(BF16))
        o_ref[half:tm, :] = jnp.zeros((tm - half, o_ref.shape[1]), F32)

    @pl.when(rows == 0)
    def _():
        o_ref[...] = jnp.zeros_like(o_ref)


def moe_down(act, wd, tile_expert, n_used, tile_rows):
    S, F = act.shape
    D = wd.shape[2]
    tm = MOE_TILE
    tn = _pick(D, (256, 128))
    nn = D // tn
    jidx = lambda t, j, te, nu, tr: jnp.where(t < nu[0], j, nn - 1)
    return pl.pallas_call(
        _moe_down_kernel,
        grid_spec=pltpu.PrefetchScalarGridSpec(
            num_scalar_prefetch=3,
            grid=(S // tm, nn),
            in_specs=[
                pl.BlockSpec((tm, F), lambda t, j, te, nu, tr: (t, 0)),
                pl.BlockSpec((1, F, tn), lambda t, j, te, nu, tr: (te[t], 0, jidx(t, j, te, nu, tr))),
            ],
            out_specs=pl.BlockSpec((tm, tn), lambda t, j, te, nu, tr: (t, j)),
        ),
        out_shape=jax.ShapeDtypeStruct((S, D), F32),
        compiler_params=_cparams("arbitrary", "arbitrary"),
        name="moe_down",
    )(tile_expert, n_used, tile_rows, act, wd)


def _combine_kernel(p1_ref, p2_ref, y_ref, gw_ref, res_ref, o_ref, y1_ref, y2_ref, sem):
    n = GATHER_ROWS

    def start(r, _):
        _row_copy(y_ref, y1_ref, sem.at[0], p1_ref[0, 0, r], r).start()
        _row_copy(y_ref, y2_ref, sem.at[1], p2_ref[0, 0, r], r).start()
        return 0

    lax.fori_loop(0, n, start, 0, unroll=DMA_UNROLL)
    pltpu.make_async_copy(y1_ref, y1_ref, sem.at[0]).wait()
    pltpu.make_async_copy(y2_ref, y2_ref, sem.at[1]).wait()
    gw = gw_ref[...]
    o_ref[...] = res_ref[...] + gw[:, 0:1] * y1_ref[...] + gw[:, 1:2] * y2_ref[...]


def moe_combine(y, pos1, pos2, gw, res):
    R, D = res.shape
    n = GATHER_ROWS
    NL = gw.shape[1]
    smem = lambda: pl.BlockSpec((1, 1, n), lambda i: (i, 0, 0), memory_space=pltpu.SMEM)
    return pl.pallas_call(
        _combine_kernel,
        grid=(R // n,),
        in_specs=[
            smem(), smem(),
            pl.BlockSpec(memory_space=pl.ANY),
            pl.BlockSpec((n, NL), lambda i: (i, 0)),
            pl.BlockSpec((n, D), lambda i: (i, 0)),
        ],
        out_specs=pl.BlockSpec((n, D), lambda i: (i, 0)),
        out_shape=jax.ShapeDtypeStruct((R, D), F32),
        scratch_shapes=[pltpu.VMEM((n, D), F32), pltpu.VMEM((n, D), F32),
                        pltpu.SemaphoreType.DMA((2,))],
        compiler_params=_cparams("arbitrary"),
        name="moe_combine",
    )(pos1.reshape(R // n, 1, n), pos2.reshape(R // n, 1, n), y, gw, res)


def moe_dispatch(idx, n_exp):
    R = idx.shape[0]
    tm = MOE_TILE
    A = 2 * R
    n_tiles = -(-A // tm) + n_exp
    e_flat = jnp.concatenate([idx[:, 0], idx[:, 1]])
    tok = jnp.concatenate([jnp.arange(R, dtype=jnp.int32)] * 2)
    onehot = (e_flat[:, None] == jnp.arange(n_exp, dtype=jnp.int32)[None, :]).astype(jnp.int32)
    csum = jnp.cumsum(onehot, axis=0)
    rank = jnp.sum((csum - onehot) * onehot, axis=1)
    counts = csum[-1]
    ptiles = (counts + tm - 1) // tm
    tile_end = jnp.cumsum(ptiles)
    tile_off = tile_end - ptiles
    pos = (tile_off[e_flat] * tm + rank).astype(jnp.int32)
    slot_token = jnp.zeros((n_tiles * tm,), jnp.int32).at[pos].set(tok)
    n_used = tile_end[-1].astype(jnp.int32)
    t = jnp.arange(n_tiles, dtype=jnp.int32)
    tile_expert = jnp.sum((t[:, None] >= tile_end[None, :]).astype(jnp.int32), axis=1)
    te_c = jnp.minimum(tile_expert, n_exp - 1)
    tile_rows = jnp.clip(counts[te_c] - (t - tile_off[te_c]) * tm, 0, tm)
    tile_rows = jnp.where(t < n_used, tile_rows, 0).astype(jnp.int32)
    last = jnp.sum((n_used - 1 >= tile_end).astype(jnp.int32))
    tile_expert = jnp.where(t < n_used, tile_expert, last).astype(jnp.int32)
    return slot_token, pos[:R], pos[R:], tile_expert, n_used.reshape(1), tile_rows


def moe_layer(h_res, g, wr_pad, wg, wu, wd, n_exp, layer):
    xn, idx, gw = router(h_res, g, wr_pad, n_exp)
    slot_token, pos1, pos2, tile_expert, n_used, tile_rows = moe_dispatch(idx[:, :2], n_exp)
    tile_expert = tile_expert + layer * n_exp
    act = moe_up(xn, slot_token, wg, wu, tile_expert, n_used, tile_rows)
    y = moe_down(act, wd, tile_expert, n_used, tile_rows)
    return moe_combine(y, pos1, pos2, gw, h_res)


def _final_kernel(x_ref, g_ref, o_ref):
    o_ref[0] = _rms(x_ref[0], g_ref[...])


def final_norm(h3, g, seq):
    B, Lp, D = h3.shape
    tm = _pick(seq, (512, 256, 128))
    return pl.pallas_call(
        _final_kernel,
        grid=(B, seq // tm),
        in_specs=[pl.BlockSpec((1, tm, D), lambda b, i: (b, i, 0)),
                  pl.BlockSpec((1, D), lambda b, i: (0, 0))],
        out_specs=pl.BlockSpec((1, tm, D), lambda b, i: (b, i, 0)),
        out_shape=jax.ShapeDtypeStruct((B, seq, D), F32),
        compiler_params=_cparams("parallel", "parallel"),
        name="final_norm",
    )(h3, g.reshape(1, D))


def kernel(x, meta_tokens, norm_mix_g, w_in, w_out, attn_norm_g, ssm_lambda_re, ssm_lambda_im, ssm_log_dt, ssm_b_re, ssm_b_im, ssm_c_re, ssm_c_im, ssm_d, ssm_w_glu, ssm_norm_g, conv_w_dw, conv_b_dw, conv_ln_g, conv_ln_b, conv_w_pw, conv_norm_g, norm_ffn_g, dense_w_gate, dense_w_up, dense_w_down, moe_w_router, moe_w_gate, moe_w_up, moe_w_down, final_norm_g):
    B, seq, D = x.shape
    n_meta = meta_tokens.shape[0]
    depth = w_in.shape[0]
    AW = attn_norm_g.shape[1]
    SW = ssm_d.shape[1]
    C = conv_b_dw.shape[1]
    n_exp = moe_w_router.shape[2]
    Lp = seq + TAIL_ROWS
    R = B * Lp
    assert seq % ATTN_TILE == 0 and n_meta <= TAIL_ROWS and AW % HEAD_DIM == 0
    assert conv_w_dw.shape[1] - 1 <= HALO and SW == C

    meta = jnp.broadcast_to(meta_tokens.astype(F32)[None], (B, n_meta, D))
    pad = jnp.zeros((B, TAIL_ROWS - n_meta, D), F32)
    h_res = jnp.concatenate([x.astype(F32), pad, meta], axis=1).reshape(R, D)
    local = jnp.arange(Lp, dtype=jnp.int32)
    keep = jnp.logical_or(local < seq, local >= Lp - n_meta).astype(F32)
    keep = jnp.broadcast_to(keep[None, :], (B, Lp)).reshape(R, 1)

    q_scale = HEAD_DIM ** -0.5 * math.log2(math.e)
    in_scale = jnp.concatenate([jnp.full((1, AW), q_scale, F32),
                                jnp.ones((1, 2 * AW + SW + 2 * C), F32)], axis=1)
    NL = 128
    bf = lambda a: a.astype(BF16)
    d_ff = moe_w_gate.shape[3]
    moe_wg = moe_w_gate.astype(F32).reshape(-1, D, d_ff)
    moe_wu = moe_w_up.astype(F32).reshape(-1, D, d_ff)
    moe_wd = moe_w_down.astype(F32).reshape(-1, d_ff, D)

    for i in range(depth):
        qkv, rest = rms_matmul(h_res, norm_mix_g[i], bf(w_in[i]), in_scale, 3 * AW)
        attn = attention(qkv.reshape(B, Lp, 3 * AW), seq, n_meta)
        bd, cd, pw = s5_params(ssm_lambda_re[i], ssm_lambda_im[i], ssm_log_dt[i], ssm_b_re[i],
                               ssm_b_im[i], ssm_c_re[i], ssm_c_im[i])
        rest3 = rest.reshape(B, Lp, SW + 2 * C)
        ssm = s5_mixer(rest3, bd, cd, pw, ssm_d[i].reshape(1, SW).astype(F32), bf(ssm_w_glu[i]),
                       ssm_norm_g[i].reshape(1, SW).astype(F32))
        conv = conv_module(rest3, SW, conv_w_dw[i].astype(F32), conv_b_dw[i].reshape(1, C).astype(F32),
                           conv_ln_g[i].reshape(1, C).astype(F32), conv_ln_b[i].reshape(1, C).astype(F32),
                           bf(conv_w_pw[i]), conv_norm_g[i].reshape(1, C).astype(F32))
        h_res = out_proj(attn.reshape(R, AW), ssm.reshape(R, SW), conv.reshape(R, C),
                         attn_norm_g[i].reshape(1, AW).astype(F32), bf(w_out[i]), h_res, keep)
        j = i // 2
        if i % 2 == 0:
            act = ffn_up(h_res, norm_ffn_g[i], bf(dense_w_gate[j]), bf(dense_w_up[j]))
            h_res = ffn_down(act, bf(dense_w_down[j]), h_res)
        else:
            wr_pad = jnp.zeros((D, NL), F32).at[:, :n_exp].set(moe_w_router[j].astype(F32))
            h_res = moe_layer(h_res, norm_ffn_g[i], wr_pad, moe_wg, moe_wu, moe_wd, n_exp, j)

    return final_norm(h_res.reshape(B, Lp, D), final_norm_g.astype(F32), seq).astype(x.dtype)
```

```python
import functools
import math

import jax
import jax.numpy as jnp
from jax import lax
from jax.experimental import pallas as pl
from jax.experimental.pallas import tpu as pltpu

F32 = jnp.float32
BF16 = jnp.bfloat16
EPS = 1e-6

HEAD_DIM = 128
SSM_GROUP_CH = 16
TAIL_ROWS = 128
ATTN_TILE = 256
ATTN_ROW_TILES = 2
ATTN_HEADS_PER_STEP = 4
SEQ_TILE = 128
SUBLANES = 8
HALO = 32
VMEM_LIMIT = 56 * 1024 * 1024
MOE_TILE = 1024
MOE_PARTS = 4
GATHER_ROWS = 256
DMA_UNROLL = 8


def _pick(n, prefs):
    for p in prefs:
        if n % p == 0:
            return p
    raise ValueError(f"no tile for {n} in {prefs}")


def _cparams(*sem):
    return pltpu.CompilerParams(dimension_semantics=sem, vmem_limit_bytes=VMEM_LIMIT)


def _dot(a, b):
    return jnp.dot(a, b, preferred_element_type=F32)


def _rms(x, g):
    ms = jnp.mean(x * x, axis=-1, keepdims=True)
    return x * lax.rsqrt(ms + EPS) * g


def _sigmoid(x):
    return 1.0 / (1.0 + jnp.exp(-x))


def _rms_matmul_kernel(x_ref, g_ref, w_ref, s_ref, oa_ref, ob_ref, xn_ref, *, na):
    j = pl.program_id(1)

    @pl.when(j == 0)
    def _():
        xn_ref[...] = _rms(x_ref[...], g_ref[...]).astype(BF16)

    acc = _dot(xn_ref[...], w_ref[...]) * s_ref[...]

    @pl.when(j < na)
    def _():
        oa_ref[...] = acc.astype(oa_ref.dtype)

    @pl.when(j >= na)
    def _():
        ob_ref[...] = acc.astype(ob_ref.dtype)


def rms_matmul(x, g, w, col_scale, n_a):
    R, D = x.shape
    N = w.shape[1]
    tm = _pick(R, (768, 512, 256, 128))
    tn = _pick(math.gcd(n_a, N - n_a), (768, 512, 384, 256, 128))
    na = n_a // tn
    return pl.pallas_call(
        functools.partial(_rms_matmul_kernel, na=na),
        grid=(R // tm, N // tn),
        in_specs=[
            pl.BlockSpec((tm, D), lambda i, j: (i, 0)),
            pl.BlockSpec((1, D), lambda i, j: (0, 0)),
            pl.BlockSpec((D, tn), lambda i, j: (0, j)),
            pl.BlockSpec((1, tn), lambda i, j: (0, j)),
        ],
        out_specs=[pl.BlockSpec((tm, tn), lambda i, j: (i, jnp.minimum(j, na - 1))),
                   pl.BlockSpec((tm, tn), lambda i, j: (i, jnp.maximum(j - na, 0)))],
        out_shape=[jax.ShapeDtypeStruct((R, n_a), BF16),
                   jax.ShapeDtypeStruct((R, N - n_a), F32)],
        scratch_shapes=[pltpu.VMEM((tm, D), BF16)],
        compiler_params=_cparams("parallel", "arbitrary"),
        name="rms_in_proj",
    )(x, g.reshape(1, D), w, col_scale)


def _sb_logits(q, kb, mask):
    z = lax.dot_general(q, kb, (((1,), (1,)), ((), ())), preferred_element_type=F32)
    neg_abs = lax.bitcast_convert_type(
        lax.bitcast_convert_type(z, jnp.uint32) | jnp.uint32(0x80000000), F32)
    lb = jnp.minimum(z, 0.0) - jnp.log2(1.0 + jnp.exp2(neg_abs))
    l1m = lb - z
    if mask is not None:
        l1m = jnp.where(mask, l1m, 0.0)
    return lb, l1m.astype(BF16), jnp.sum(l1m, axis=-1, keepdims=True)


def _sb_weights(lb, l1m, vb, later, car, mask):
    after = _dot(l1m, later)
    w = jnp.exp2(lb + after + car)
    if mask is not None:
        w = jnp.where(mask, w, 0.0)
    return _dot(w.astype(BF16), vb)


def _sb_block(q, kb, vb, later, car, mask):
    lb, l1m, rs = _sb_logits(q, kb, mask)
    return _sb_weights(lb, l1m, vb, later, car, mask), rs


def _attn_kernel(q_ref, k_ref, v_ref, tri_ref, trit_ref, o_ref, acc_ref, car_ref,
                 *, seq, n_meta, m, hp):
    i = pl.program_id(2)
    tk = ATTN_TILE
    tq = m * tk
    tb = TAIL_ROWS
    dh = HEAD_DIM
    n_real = seq // tq
    first_meta = tb - n_meta

    def step(r0, nrows, koff, nk, tri_ext, mask):
        for h in range(hp):
            cs = slice(h * dh, (h + 1) * dh)
            pv, rs = _sb_block(q_ref[0, r0:r0 + nrows, cs], k_ref[0, pl.ds(koff, nk), cs],
                               v_ref[0, pl.ds(koff, nk), cs], tri_ext,
                               car_ref[h, r0:r0 + nrows, :], mask)
            acc_ref[h, r0:r0 + nrows, :] += pv
            car_ref[h, r0:r0 + nrows, :] += rs

    acc_ref[...] = jnp.zeros_like(acc_ref)
    car_ref[...] = jnp.zeros_like(car_ref)

    @pl.when(i < n_real)
    def _():
        for d in reversed(range(m)):
            nrows = tq - d * tk
            rr = lax.broadcasted_iota(jnp.int32, (nrows, tk), 0)
            cc = lax.broadcasted_iota(jnp.int32, (nrows, tk), 1)
            step(d * tk, nrows, pl.multiple_of((i * m + d) * tk, tk), tk, tri_ref[...], cc < rr)

        def body(n, _):
            step(0, tq, pl.multiple_of((i * m - 1 - n) * tk, tk), tk, tri_ref[...], None)
            return 0

        lax.fori_loop(0, i * m, body, 0)
        ct = lax.broadcasted_iota(jnp.int32, (tq, tb), 1)
        step(0, tq, seq, tb, trit_ref[...], ct >= first_meta)
        for h in range(hp):
            o_ref[0, :, h * dh:(h + 1) * dh] = acc_ref[h]

    @pl.when(i == n_real)
    def _():
        rr = lax.broadcasted_iota(jnp.int32, (tb, tb), 0)
        cc = lax.broadcasted_iota(jnp.int32, (tb, tb), 1)
        step(0, tb, seq, tb, trit_ref[...], jnp.logical_and(cc < rr, cc >= first_meta))
        o_ref[...] = jnp.zeros_like(o_ref)
        for h in range(hp):
            o_ref[0, 0:tb, h * dh:(h + 1) * dh] = acc_ref[h, 0:tb, :]


def attention(qkv, seq, n_meta):
    B, Lp, W3 = qkv.shape
    H = W3 // (3 * HEAD_DIM)
    tk = ATTN_TILE
    m = ATTN_ROW_TILES if seq % (ATTN_ROW_TILES * tk) == 0 else 1
    hp = ATTN_HEADS_PER_STEP if H % ATTN_HEADS_PER_STEP == 0 else 1
    tq = m * tk
    wb = hp * HEAD_DIM
    ng = H // hp
    ti = jnp.arange(tk, dtype=jnp.int32)
    tri = (ti[:, None] > ti[None, :]).astype(BF16)
    trit = tri[:TAIL_ROWS, :TAIL_ROWS]
    return pl.pallas_call(
        functools.partial(_attn_kernel, seq=seq, n_meta=n_meta, m=m, hp=hp),
        grid=(B, ng, seq // tq + 1),
        in_specs=[
            pl.BlockSpec((1, tq, wb), lambda b, g, i: (b, i, g)),
            pl.BlockSpec((1, Lp, wb), lambda b, g, i: (b, 0, ng + g)),
            pl.BlockSpec((1, Lp, wb), lambda b, g, i: (b, 0, 2 * ng + g)),
            pl.BlockSpec(tri.shape, lambda b, g, i: (0, 0)),
            pl.BlockSpec(trit.shape, lambda b, g, i: (0, 0)),
        ],
        out_specs=pl.BlockSpec((1, tq, wb), lambda b, g, i: (b, i, g)),
        out_shape=jax.ShapeDtypeStruct((B, Lp, H * HEAD_DIM), F32),
        scratch_shapes=[pltpu.VMEM((hp, tq, HEAD_DIM), F32), pltpu.VMEM((hp, tq, 1), F32)],
        compiler_params=_cparams("parallel", "parallel", "arbitrary"),
        name="sb_attention",
    )(qkv, qkv, qkv, tri, trit)


def _gelu_tanh(x):
    return 0.5 * x * (1.0 + jnp.tanh(math.sqrt(2.0 / math.pi) * (x + 0.044715 * (x * x * x))))


def _s5_kernel(u_ref, bd_ref, cd_ref, pw_ref, d_ref, wglu_ref, g_ref, o_ref,
               carry_ref, st_ref, y_ref, *, nh, cw, cs):
    tc = SEQ_TILE

    @pl.when(pl.program_id(1) == 0)
    def _():
        carry_ref[...] = jnp.zeros_like(carry_ref)

    u = u_ref[0]
    ub = u.astype(BF16)
    for c in range(nh):
        st_ref[...] = _dot(ub[:, c * cw:(c + 1) * cw], bd_ref[c])

        def tile(r, carry, c=c):
            cre, cim = carry
            off = pl.multiple_of(r * SUBLANES, SUBLANES)
            xre = st_ref[pl.ds(off, SUBLANES), 0:cs]
            xim = st_ref[pl.ds(off, SUBLANES), cs:2 * cs]
            for kk, sh in enumerate((1, 2, 4)):
                pr = pw_ref[c, kk, :, 0:cs]
                pi = pw_ref[c, kk, :, cs:2 * cs]
                sre = pltpu.roll(xre, sh, axis=0)
                sim = pltpu.roll(xim, sh, axis=0)
                xre, xim = xre + pr * sre - pi * sim, xim + pr * sim + pi * sre
            pr = pw_ref[c, 3, :, 0:cs]
            pi = pw_ref[c, 3, :, cs:2 * cs]
            xre, xim = xre + pr * cre - pi * cim, xim + pr * cim + pi * cre
            st_ref[pl.ds(off, SUBLANES), 0:cs] = xre
            st_ref[pl.ds(off, SUBLANES), cs:2 * cs] = xim
            return (jnp.broadcast_to(xre[SUBLANES - 1:SUBLANES, :], (SUBLANES, cs)),
                    jnp.broadcast_to(xim[SUBLANES - 1:SUBLANES, :], (SUBLANES, cs)))

        cre, cim = lax.fori_loop(0, tc // SUBLANES, tile,
                                 (carry_ref[c, :, 0:cs], carry_ref[c, :, cs:2 * cs]))
        carry_ref[c, :, 0:cs] = cre
        carry_ref[c, :, cs:2 * cs] = cim
        y_ref[:, c * cw:(c + 1) * cw] = _dot(st_ref[...].astype(BF16), cd_ref[c])

    y = y_ref[...] + d_ref[...] * u
    g = _gelu_tanh(y)
    out = g * _sigmoid(_dot(g.astype(BF16), wglu_ref[...]))
    o_ref[0] = _rms(out, g_ref[...]).astype(BF16)


def s5_mixer(rest, bd, cd, pw, d, wglu, g):
    B, Lp, _ = rest.shape
    nh, cw, cs2 = bd.shape
    cs = cs2 // 2
    SW = nh * cw
    tc = SEQ_TILE
    nblk = Lp // tc
    seq_blk = lambda b, s: (b, (s + nblk - 1) % nblk, 0)
    full = lambda *shape: pl.BlockSpec(shape, lambda b, s: (0,) * len(shape))
    return pl.pallas_call(
        functools.partial(_s5_kernel, nh=nh, cw=cw, cs=cs),
        grid=(B, nblk),
        in_specs=[
            pl.BlockSpec((1, tc, SW), seq_blk),
            full(nh, cw, 2 * cs), full(nh, 2 * cs, cw), full(nh, 4, SUBLANES, 2 * cs),
            full(1, SW), full(SW, SW), full(1, SW),
        ],
        out_specs=pl.BlockSpec((1, tc, SW), seq_blk),
        out_shape=jax.ShapeDtypeStruct((B, Lp, SW), BF16),
        scratch_shapes=[pltpu.VMEM((nh, SUBLANES, 2 * cs), F32),
                        pltpu.VMEM((tc, 2 * cs), F32),
                        pltpu.VMEM((tc, SW), F32)],
        compiler_params=_cparams("parallel", "arbitrary"),
        name="s5_mixer",
    )(rest, bd, cd, pw, d, wglu, g)


def s5_params(lam_re, lam_im, log_dt, b_re, b_im, c_re, c_im):
    G, P = lam_re.shape
    Hc = SSM_GROUP_CH
    dt = jnp.exp(log_dt.astype(F32))[:, None]
    lr = lam_re.astype(F32)
    li = lam_im.astype(F32)
    mag = jnp.exp(lr * dt)
    ang = li * dt
    ab_re = mag * jnp.cos(ang)
    ab_im = mag * jnp.sin(ang)
    den = lr * lr + li * li
    nr = ab_re - 1.0
    ni = ab_im
    coef_re = (nr * lr + ni * li) / den
    coef_im = (ni * lr - nr * li) / den
    br = b_re.astype(F32)
    bi = b_im.astype(F32)
    bbar_re = coef_re[..., None] * br - coef_im[..., None] * bi
    bbar_im = coef_re[..., None] * bi + coef_im[..., None] * br

    SW = G * Hc
    cw = min(SW, 256)
    gc = cw // Hc
    nh = G // gc
    cs = gc * P
    eye = jnp.eye(gc, dtype=F32)

    def in_blockdiag(m):
        m = m.reshape(nh, gc, P, Hc)
        return jnp.einsum('ngph,gk->nghkp', m, eye).reshape(nh, gc * Hc, gc * P)

    def out_blockdiag(m):
        m = m.reshape(nh, gc, Hc, P)
        return jnp.einsum('nghp,gk->ngpkh', m, eye).reshape(nh, gc * P, gc * Hc)

    bd = jnp.concatenate([in_blockdiag(bbar_re), in_blockdiag(bbar_im)], axis=-1).astype(BF16)
    cd = jnp.concatenate([out_blockdiag(c_re.astype(F32)),
                          -out_blockdiag(c_im.astype(F32))], axis=1).astype(BF16)

    rows = jnp.arange(SUBLANES, dtype=F32)[:, None, None]

    def power(e):
        m = jnp.exp(e * (lr * dt)[None])
        a = e * ang[None]
        return m * jnp.cos(a), m * jnp.sin(a)

    tabs = []
    for sh in (1, 2, 4):
        pr, pi = power(jnp.full((SUBLANES, 1, 1), float(sh), F32))
        keep = rows >= sh
        tabs.append((jnp.where(keep, pr, 0.0), jnp.where(keep, pi, 0.0)))
    tabs.append(power(rows + 1.0))
    pw = jnp.stack([jnp.concatenate([pr.reshape(SUBLANES, nh, cs), pi.reshape(SUBLANES, nh, cs)],
                                    axis=-1) for pr, pi in tabs], axis=0)
    pw = pw.transpose(2, 0, 1, 3)
    return bd, cd, pw


def _conv_kernel(val_ref, gate_ref, hval_ref, hgate_ref, wdw_ref, bdw_ref, lng_ref, lnb_ref,
                 wpw_ref, ng_ref, o_ref, hbuf_ref, *, taps, nblk):
    tc = SEQ_TILE
    halo = hval_ref[0] * _sigmoid(hgate_ref[0])
    halo = jnp.where(pl.program_id(1) == nblk - 1, 0.0, halo)
    hbuf_ref[0:HALO, :] = halo
    hbuf_ref[HALO:HALO + tc, :] = val_ref[0] * _sigmoid(gate_ref[0])
    acc = jnp.broadcast_to(bdw_ref[...], o_ref.shape[1:]).astype(F32)
    base = HALO - (taps - 1)
    for b in range(SUBLANES):
        js = [j for j in range(base, base + taps) if j % SUBLANES == b]
        if not js:
            continue
        rows = tc if b == 0 else tc + SUBLANES
        g = None
        for j in js:
            term = wdw_ref[j - base:j - base + 1, :] * hbuf_ref[j - b:j - b + rows, :]
            g = term if g is None else g + term
        acc = acc + g[b:b + tc, :]
    mu = jnp.mean(acc, axis=-1, keepdims=True)
    xc = acc - mu
    var = jnp.mean(xc * xc, axis=-1, keepdims=True)
    y = xc * lax.rsqrt(var + EPS) * lng_ref[...] + lnb_ref[...]
    y = y * _sigmoid(y)
    out = _dot(y.astype(BF16), wpw_ref[...])
    o_ref[0] = _rms(out, ng_ref[...]).astype(BF16)


def conv_module(rest, col0, wdw, bdw, lng, lnb, wpw, ng):
    B, Lp, _ = rest.shape
    taps, C = wdw.shape
    tc = SEQ_TILE
    nblk = Lp // tc
    vb = col0 // C
    per = tc // HALO
    prev_tail = lambda b, p: ((p + nblk - 1) % nblk) * per + per - 1
    full = lambda *shape: pl.BlockSpec(shape, lambda b, p: (0,) * len(shape))
    return pl.pallas_call(
        functools.partial(_conv_kernel, taps=taps, nblk=nblk),
        grid=(B, nblk),
        in_specs=[
            pl.BlockSpec((1, tc, C), lambda b, p: (b, p, vb)),
            pl.BlockSpec((1, tc, C), lambda b, p: (b, p, vb + 1)),
            pl.BlockSpec((1, HALO, C), lambda b, p: (b, prev_tail(b, p), vb)),
            pl.BlockSpec((1, HALO, C), lambda b, p: (b, prev_tail(b, p), vb + 1)),
            full(taps, C), full(1, C), full(1, C), full(1, C), full(C, C), full(1, C),
        ],
        out_specs=pl.BlockSpec((1, tc, C), lambda b, p: (b, p, 0)),
        out_shape=jax.ShapeDtypeStruct((B, Lp, C), BF16),
        scratch_shapes=[pltpu.VMEM((HALO + tc, C), F32)],
        compiler_params=_cparams("parallel", "parallel"),
        name="conv_module",
    )(rest, rest, rest, rest, wdw, bdw, lng, lnb, wpw, ng)


def _outproj_kernel(attn_ref, ssm_ref, conv_ref, ag_ref, w_ref, res_ref, keep_ref, o_ref,
                    an_ref, *, aw, sw):
    @pl.when(pl.program_id(1) == 0)
    def _():
        an_ref[...] = _rms(attn_ref[...], ag_ref[...]).astype(BF16)

    acc = (_dot(an_ref[...], w_ref[0:aw, :]) + _dot(ssm_ref[...], w_ref[aw:aw + sw, :])
           + _dot(conv_ref[...], w_ref[aw + sw:, :]))
    o_ref[...] = res_ref[...] + jnp.where(keep_ref[...] > 0.0, acc, 0.0)


def out_proj(attn, ssm, conv, ag, w, res, keep):
    R, AW = attn.shape
    SW = ssm.shape[1]
    C = conv.shape[1]
    D = w.shape[1]
    tm = _pick(R, (768, 512, 256, 128))
    tn = _pick(D, (1024, 512, 256, 128))
    return pl.pallas_call(
        functools.partial(_outproj_kernel, aw=AW, sw=SW),
        grid=(R // tm, D // tn),
        in_specs=[
            pl.BlockSpec((tm, AW), lambda i, j: (i, 0)),
            pl.BlockSpec((tm, SW), lambda i, j: (i, 0)),
            pl.BlockSpec((tm, C), lambda i, j: (i, 0)),
            pl.BlockSpec((1, AW), lambda i, j: (0, 0)),
            pl.BlockSpec((AW + SW + C, tn), lambda i, j: (0, j)),
            pl.BlockSpec((tm, tn), lambda i, j: (i, j)),
            pl.BlockSpec((tm, 1), lambda i, j: (i, 0)),
        ],
        out_specs=pl.BlockSpec((tm, tn), lambda i, j: (i, j)),
        out_shape=jax.ShapeDtypeStruct((R, D), F32),
        scratch_shapes=[pltpu.VMEM((tm, AW), BF16)],
        compiler_params=_cparams("parallel", "arbitrary"),
        name="out_proj",
    )(attn, ssm, conv, ag, w, res, keep)


def _ffn_up_kernel(x_ref, g_ref, wg_ref, wu_ref, o_ref, xn_ref):
    @pl.when(pl.program_id(1) == 0)
    def _():
        xn_ref[...] = _rms(x_ref[...], g_ref[...]).astype(BF16)

    a = _dot(xn_ref[...], wg_ref[...])
    o_ref[...] = (a * _sigmoid(a) * _dot(xn_ref[...], wu_ref[...])).astype(BF16)


def ffn_up(x, g, wg, wu):
    R, D = x.shape
    F = wg.shape[1]
    tm = _pick(R, (768, 512, 256, 128))
    tf = _pick(F, (512, 256, 128))
    return pl.pallas_call(
        _ffn_up_kernel,
        grid=(R // tm, F // tf),
        in_specs=[
            pl.BlockSpec((tm, D), lambda i, j: (i, 0)),
            pl.BlockSpec((1, D), lambda i, j: (0, 0)),
            pl.BlockSpec((D, tf), lambda i, j: (0, j)),
            pl.BlockSpec((D, tf), lambda i, j: (0, j)),
        ],
        out_specs=pl.BlockSpec((tm, tf), lambda i, j: (i, j)),
        out_shape=jax.ShapeDtypeStruct((R, F), BF16),
        scratch_shapes=[pltpu.VMEM((tm, D), BF16)],
        compiler_params=_cparams("parallel", "arbitrary"),
        name="ffn_up",
    )(x, g.reshape(1, D), wg, wu)


def _ffn_down_kernel(a_ref, w_ref, res_ref, o_ref):
    o_ref[...] = res_ref[...] + _dot(a_ref[...], w_ref[...])


def ffn_down(act, w, res):
    R, F = act.shape
    D = w.shape[1]
    tm = _pick(R, (768, 512, 256, 128))
    tn = _pick(D, (512, 256, 128))
    return pl.pallas_call(
        _ffn_down_kernel,
        grid=(R // tm, D // tn),
        in_specs=[
            pl.BlockSpec((tm, F), lambda i, j: (i, 0)),
            pl.BlockSpec((F, tn), lambda i, j: (0, j)),
            pl.BlockSpec((tm, tn), lambda i, j: (i, j)),
        ],
        out_specs=pl.BlockSpec((tm, tn), lambda i, j: (i, j)),
        out_shape=jax.ShapeDtypeStruct((R, D), F32),
        compiler_params=_cparams("parallel", "parallel"),
        name="ffn_down",
    )(act, w, res)


def _router_kernel(x_ref, g_ref, wr_ref, xn_ref, idx_ref, gw_ref, *, n_exp):
    xn = _rms(x_ref[...], g_ref[...])
    xn_ref[...] = xn
    logits = jnp.dot(xn, wr_ref[...], preferred_element_type=F32,
                     precision=lax.Precision.HIGHEST)
    lane = lax.broadcasted_iota(jnp.int32, logits.shape, 1)
    lanef = lane.astype(F32)
    big = float(logits.shape[1])
    logits = jnp.where(lane < n_exp, logits, -jnp.inf)
    m1 = jnp.max(logits, axis=-1, keepdims=True)
    i1 = jnp.min(jnp.where(logits == m1, lanef, big), axis=-1, keepdims=True)
    rest = jnp.where(lanef == i1, -jnp.inf, logits)
    m2 = jnp.max(rest, axis=-1, keepdims=True)
    i2 = jnp.min(jnp.where(rest == m2, lanef, big), axis=-1, keepdims=True)
    e = jnp.exp(m2 - m1)
    w1 = 1.0 / (1.0 + e)
    w2 = e / (1.0 + e)
    idx_ref[...] = jnp.where(lane == 0, i1, jnp.where(lane == 1, i2, 0.0)).astype(jnp.int32)
    gw_ref[...] = jnp.where(lane == 0, w1, jnp.where(lane == 1, w2, 0.0))


def router(x, g, wr_pad, n_exp):
    R, D = x.shape
    tm = _pick(R, (512, 256, 128))
    NL = wr_pad.shape[1]
    return pl.pallas_call(
        functools.partial(_router_kernel, n_exp=n_exp),
        grid=(R // tm,),
        in_specs=[
            pl.BlockSpec((tm, D), lambda i: (i, 0)),
            pl.BlockSpec((1, D), lambda i: (0, 0)),
            pl.BlockSpec((D, NL), lambda i: (0, 0)),
        ],
        out_specs=[
            pl.BlockSpec((tm, D), lambda i: (i, 0)),
            pl.BlockSpec((tm, NL), lambda i: (i, 0)),
            pl.BlockSpec((tm, NL), lambda i: (i, 0)),
        ],
        out_shape=[jax.ShapeDtypeStruct((R, D), F32),
                   jax.ShapeDtypeStruct((R, NL), jnp.int32),
                   jax.ShapeDtypeStruct((R, NL), F32)],
        compiler_params=_cparams("parallel"),
        name="moe_router",
    )(x, g.reshape(1, D), wr_pad)


def _row_copy(src_ref, dst_ref, sem, s, d):
    return pltpu.make_async_copy(src_ref.at[pl.ds(s, 1), :], dst_ref.at[pl.ds(d, 1), :], sem)


def _moe_up_kernel(te_ref, nu_ref, tr_ref, idx0_ref, idxn_ref, xn_ref, wg_ref, wu_ref, o_ref,
                   xf_ref, xb_ref, sem):
    t = pl.program_id(0)
    tm = MOE_TILE
    part = tm // MOE_PARTS
    n_used = nu_ref[0]
    rows = tr_ref[t]

    def start_gather(idx_ref, slot):
        def body(r, _):
            pltpu.make_async_copy(xn_ref.at[pl.ds(idx_ref[0, 0, r], 1), :],
                                  xf_ref.at[slot, pl.ds(r, 1), :], sem.at[slot]).start()
            return 0

        lax.fori_loop(0, tm, body, 0, unroll=DMA_UNROLL)

    @pl.when(pl.program_id(1) == 0)
    def _():
        slot = t % 2

        @pl.when(t == 0)
        def _():
            start_gather(idx0_ref, 0)

        @pl.when(t < n_used)
        def _():
            pltpu.make_async_copy(xf_ref.at[slot], xf_ref.at[slot], sem.at[slot]).wait()
            xb_ref[...] = xf_ref[slot].astype(BF16)

        @pl.when(t + 1 < n_used)
        def _():
            start_gather(idxn_ref, 1 - slot)

    def swiglu_rows(n):
        a = _dot(xb_ref[0:n, :], wg_ref[0].astype(BF16))
        return (a * _sigmoid(a) * _dot(xb_ref[0:n, :], wu_ref[0].astype(BF16))).astype(BF16)

    for k in range(1, MOE_PARTS + 1):
        n = k * part

        @pl.when(jnp.logical_and(rows > n - part, rows <= n))
        def _(n=n):
            o_ref[0:n, :] = swiglu_rows(n)
            if n < tm:
                o_ref[n:tm, :] = jnp.zeros((tm - n, o_ref.shape[1]), BF16)

    @pl.when(rows == 0)
    def _():
        o_ref[...] = jnp.zeros_like(o_ref)


def moe_up(xn, slot_token, wg, wu, tile_expert, n_used, tile_rows):
    D = xn.shape[1]
    S = slot_token.shape[0]
    F = wg.shape[2]
    tm = MOE_TILE
    nt = S // tm
    tf = _pick(F, (512, 256, 128))
    nf = F // tf
    fidx = lambda t, f, te, nu, tr: jnp.where(t < nu[0], f, nf - 1)
    idx3 = slot_token.reshape(nt, 1, tm)
    return pl.pallas_call(
        _moe_up_kernel,
        grid_spec=pltpu.PrefetchScalarGridSpec(
            num_scalar_prefetch=3,
            grid=(nt, nf),
            in_specs=[
                pl.BlockSpec((1, 1, tm), lambda t, f, te, nu, tr: (0, 0, 0), memory_space=pltpu.SMEM),
                pl.BlockSpec((1, 1, tm), lambda t, f, te, nu, tr: (jnp.minimum(t + 1, nt - 1), 0, 0),
                             memory_space=pltpu.SMEM),
                pl.BlockSpec(memory_space=pl.ANY),
                pl.BlockSpec((1, D, tf), lambda t, f, te, nu, tr: (te[t], 0, fidx(t, f, te, nu, tr))),
                pl.BlockSpec((1, D, tf), lambda t, f, te, nu, tr: (te[t], 0, fidx(t, f, te, nu, tr))),
            ],
            out_specs=pl.BlockSpec((tm, tf), lambda t, f, te, nu, tr: (t, f)),
            scratch_shapes=[pltpu.VMEM((2, tm, D), F32), pltpu.VMEM((tm, D), BF16),
                            pltpu.SemaphoreType.DMA((2,))],
        ),
        out_shape=jax.ShapeDtypeStruct((S, F), BF16),
        compiler_params=_cparams("arbitrary", "arbitrary"),
        name="moe_gather_up",
    )(tile_expert, n_used, tile_rows, idx3, idx3, xn, wg, wu)


def _moe_down_kernel(te_ref, nu_ref, tr_ref, a_ref, w_ref, o_ref):
    tm = MOE_TILE
    part = tm // MOE_PARTS
    rows = tr_ref[pl.program_id(0)]

    for k in range(1, MOE_PARTS + 1):
        n = k * part

        @pl.when(jnp.logical_and(rows > n - part, rows <= n))
        def _(n=n):
            o_ref[0:n, :] = _dot(a_ref[0:n, :], w_ref[0].astype(BF16))
            if n < tm:
                o_ref[n:tm, :] = jnp.zeros((tm - n, o_ref.shape[1]), F32)

    @pl.when(rows == 0)
    def _():
        o_ref[...] = jnp.zeros_like(o_ref)


def moe_down(act, wd, tile_expert, n_used, tile_rows):
    S, F = act.shape
    D = wd.shape[2]
    tm = MOE_TILE
    tn = _pick(D, (256, 128))
    nn = D // tn
    jidx = lambda t, j, te, nu, tr: jnp.where(t < nu[0], j, nn - 1)
    return pl.pallas_call(
        _moe_down_kernel,
        grid_spec=pltpu.PrefetchScalarGridSpec(
            num_scalar_prefetch=3,
            grid=(S // tm, nn),
            in_specs=[
                pl.BlockSpec((tm, F), lambda t, j, te, nu, tr: (t, 0)),
                pl.BlockSpec((1, F, tn), lambda t, j, te, nu, tr: (te[t], 0, jidx(t, j, te, nu, tr))),
            ],
            out_specs=pl.BlockSpec((tm, tn), lambda t, j, te, nu, tr: (t, j)),
        ),
        out_shape=jax.ShapeDtypeStruct((S, D), F32),
        compiler_params=_cparams("arbitrary", "arbitrary"),
        name="moe_down",
    )(tile_expert, n_used, tile_rows, act, wd)


def _combine_kernel(p1_ref, p2_ref, y_ref, gw_ref, res_ref, o_ref, y1_ref, y2_ref, sem):
    n = GATHER_ROWS

    def start(r, _):
        _row_copy(y_ref, y1_ref, sem.at[0], p1_ref[0, 0, r], r).start()
        _row_copy(y_ref, y2_ref, sem.at[1], p2_ref[0, 0, r], r).start()
        return 0

    lax.fori_loop(0, n, start, 0, unroll=DMA_UNROLL)
    pltpu.make_async_copy(y1_ref, y1_ref, sem.at[0]).wait()
    pltpu.make_async_copy(y2_ref, y2_ref, sem.at[1]).wait()
    gw = gw_ref[...]
    o_ref[...] = res_ref[...] + gw[:, 0:1] * y1_ref[...] + gw[:, 1:2] * y2_ref[...]


def moe_combine(y, pos1, pos2, gw, res):
    R, D = res.shape
    n = GATHER_ROWS
    NL = gw.shape[1]
    smem = lambda: pl.BlockSpec((1, 1, n), lambda i: (i, 0, 0), memory_space=pltpu.SMEM)
    return pl.pallas_call(
        _combine_kernel,
        grid=(R // n,),
        in_specs=[
            smem(), smem(),
            pl.BlockSpec(memory_space=pl.ANY),
            pl.BlockSpec((n, NL), lambda i: (i, 0)),
            pl.BlockSpec((n, D), lambda i: (i, 0)),
        ],
        out_specs=pl.BlockSpec((n, D), lambda i: (i, 0)),
        out_shape=jax.ShapeDtypeStruct((R, D), F32),
        scratch_shapes=[pltpu.VMEM((n, D), F32), pltpu.VMEM((n, D), F32),
                        pltpu.SemaphoreType.DMA((2,))],
        compiler_params=_cparams("arbitrary"),
        name="moe_combine",
    )(pos1.reshape(R // n, 1, n), pos2.reshape(R // n, 1, n), y, gw, res)


def moe_dispatch(idx, n_exp):
    R = idx.shape[0]
    tm = MOE_TILE
    A = 2 * R
    n_tiles = -(-A // tm) + n_exp
    e_flat = jnp.concatenate([idx[:, 0], idx[:, 1]])
    tok = jnp.concatenate([jnp.arange(R, dtype=jnp.int32)] * 2)
    onehot = (e_flat[:, None] == jnp.arange(n_exp, dtype=jnp.int32)[None, :]).astype(jnp.int32)
    csum = jnp.cumsum(onehot, axis=0)
    rank = jnp.sum((csum - onehot) * onehot, axis=1)
    counts = csum[-1]
    ptiles = (counts + tm - 1) // tm
    tile_end = jnp.cumsum(ptiles)
    tile_off = tile_end - ptiles
    pos = (tile_off[e_flat] * tm + rank).astype(jnp.int32)
    slot_token = jnp.zeros((n_tiles * tm,), jnp.int32).at[pos].set(tok)
    n_used = tile_end[-1].astype(jnp.int32)
    t = jnp.arange(n_tiles, dtype=jnp.int32)
    tile_expert = jnp.sum((t[:, None] >= tile_end[None, :]).astype(jnp.int32), axis=1)
    te_c = jnp.minimum(tile_expert, n_exp - 1)
    tile_rows = jnp.clip(counts[te_c] - (t - tile_off[te_c]) * tm, 0, tm)
    tile_rows = jnp.where(t < n_used, tile_rows, 0).astype(jnp.int32)
    last = jnp.sum((n_used - 1 >= tile_end).astype(jnp.int32))
    tile_expert = jnp.where(t < n_used, tile_expert, last).astype(jnp.int32)
    return slot_token, pos[:R], pos[R:], tile_expert, n_used.reshape(1), tile_rows


def moe_layer(h_res, g, wr_pad, wg, wu, wd, n_exp, layer):
    xn, idx, gw = router(h_res, g, wr_pad, n_exp)
    slot_token, pos1, pos2, tile_expert, n_used, tile_rows = moe_dispatch(idx[:, :2], n_exp)
    tile_expert = tile_expert + layer * n_exp
    act = moe_up(xn, slot_token, wg, wu, tile_expert, n_used, tile_rows)
    y = moe_down(act, wd, tile_expert, n_used, tile_rows)
    return moe_combine(y, pos1, pos2, gw, h_res)


def _final_kernel(x_ref, g_ref, o_ref):
    o_ref[0] = _rms(x_ref[0], g_ref[...])


def final_norm(h3, g, seq):
    B, Lp, D = h3.shape
    tm = _pick(seq, (512, 256, 128))
    return pl.pallas_call(
        _final_kernel,
        grid=(B, seq // tm),
        in_specs=[pl.BlockSpec((1, tm, D), lambda b, i: (b, i, 0)),
                  pl.BlockSpec((1, D), lambda b, i: (0, 0))],
        out_specs=pl.BlockSpec((1, tm, D), lambda b, i: (b, i, 0)),
        out_shape=jax.ShapeDtypeStruct((B, seq, D), F32),
        compiler_params=_cparams("parallel", "parallel"),
        name="final_norm",
    )(h3, g.reshape(1, D))


def kernel(x, meta_tokens, norm_mix_g, w_in, w_out, attn_norm_g, ssm_lambda_re, ssm_lambda_im, ssm_log_dt, ssm_b_re, ssm_b_im, ssm_c_re, ssm_c_im, ssm_d, ssm_w_glu, ssm_norm_g, conv_w_dw, conv_b_dw, conv_ln_g, conv_ln_b, conv_w_pw, conv_norm_g, norm_ffn_g, dense_w_gate, dense_w_up, dense_w_down, moe_w_router, moe_w_gate, moe_w_up, moe_w_down, final_norm_g):
    B, seq, D = x.shape
    n_meta = meta_tokens.shape[0]
    depth = w_in.shape[0]
    AW = attn_norm_g.shape[1]
    SW = ssm_d.shape[1]
    C = conv_b_dw.shape[1]
    n_exp = moe_w_router.shape[2]
    Lp = seq + TAIL_ROWS
    R = B * Lp
    assert seq % ATTN_TILE == 0 and n_meta <= TAIL_ROWS and AW % HEAD_DIM == 0
    assert conv_w_dw.shape[1] - 1 <= HALO and SW == C

    meta = jnp.broadcast_to(meta_tokens.astype(F32)[None], (B, n_meta, D))
    pad = jnp.zeros((B, TAIL_ROWS - n_meta, D), F32)
    h_res = jnp.concatenate([x.astype(F32), pad, meta], axis=1).reshape(R, D)
    local = jnp.arange(Lp, dtype=jnp.int32)
    keep = jnp.logical_or(local < seq, local >= Lp - n_meta).astype(F32)
    keep = jnp.broadcast_to(keep[None, :], (B, Lp)).reshape(R, 1)

    q_scale = HEAD_DIM ** -0.5 * math.log2(math.e)
    in_scale = jnp.concatenate([jnp.full((1, AW), q_scale, F32),
                                jnp.ones((1, 2 * AW + SW + 2 * C), F32)], axis=1)
    NL = 128
    bf = lambda a: a.astype(BF16)
    d_ff = moe_w_gate.shape[3]
    moe_wg = moe_w_gate.astype(F32).reshape(-1, D, d_ff)
    moe_wu = moe_w_up.astype(F32).reshape(-1, D, d_ff)
    moe_wd = moe_w_down.astype(F32).reshape(-1, d_ff, D)

    for i in range(depth):
        qkv, rest = rms_matmul(h_res, norm_mix_g[i], bf(w_in[i]), in_scale, 3 * AW)
        attn = attention(qkv.reshape(B, Lp, 3 * AW), seq, n_meta)
        bd, cd, pw = s5_params(ssm_lambda_re[i], ssm_lambda_im[i], ssm_log_dt[i], ssm_b_re[i],
                               ssm_b_im[i], ssm_c_re[i], ssm_c_im[i])
        rest3 = rest.reshape(B, Lp, SW + 2 * C)
        ssm = s5_mixer(rest3, bd, cd, pw, ssm_d[i].reshape(1, SW).astype(F32), bf(ssm_w_glu[i]),
                       ssm_norm_g[i].reshape(1, SW).astype(F32))
        conv = conv_module(rest3, SW, conv_w_dw[i].astype(F32), conv_b_dw[i].reshape(1, C).astype(F32),
                           conv_ln_g[i].reshape(1, C).astype(F32), conv_ln_b[i].reshape(1, C).astype(F32),
                           bf(conv_w_pw[i]), conv_norm_g[i].reshape(1, C).astype(F32))
        h_res = out_proj(attn.reshape(R, AW), ssm.reshape(R, SW), conv.reshape(R, C),
                         attn_norm_g[i].reshape(1, AW).astype(F32), bf(w_out[i]), h_res, keep)
        j = i // 2
        if i % 2 == 0:
            act = ffn_up(h_res, norm_ffn_g[i], bf(dense_w_gate[j]), bf(dense_w_up[j]))
            h_res = ffn_down(act, bf(dense_w_down[j]), h_res)
        else:
            wr_pad = jnp.zeros((D, NL), F32).at[:, :n_exp].set(moe_w_router[j].astype(F32))
            h_res = moe_layer(h_res, norm_ffn_g[i], wr_pad, moe_wg, moe_wu, moe_wd, n_exp, j)

    return final_norm(h_res.reshape(B, Lp, D), final_norm_g.astype(F32), seq).astype(x.dtype)
```
